```python
import math, functools
import jax, jax.numpy as jnp
from jax import lax
import numpy as np

D_MODEL = 1024
BATCH = 4
SEQ = 4096
DEPTH = 1
DEC_BATCH = 128
DEC_SEQ = 1
PAST_LEN = 8192
PAGE_SIZE = 128

N_HEADS = 8
N_KV_HEADS = 4
GQA_GROUP = N_HEADS // N_KV_HEADS
HD = 64
QK_DIM = 2 * HD
V_DIM = 2 * HD
ATTN_WIDTH = N_HEADS * V_DIM
ROT_DIM = HD // 4
ROPE_THETA = 500000.0
Q_BLOCK = 128
SSM_WIDTH = D_MODEL
SSM_GROUP = 16
SSM_GROUPS = SSM_WIDTH // SSM_GROUP
SSM_STATE = 64
SSM_CHUNK = 128
D_FF = 4 * D_MODEL
ALPHA = (2.0 * DEPTH) ** 0.25
BETA = (8.0 * DEPTH) ** -0.25
LN_EPS = 1e-5
Q_COLS = N_HEADS * QK_DIM
K_COLS = N_KV_HEADS * QK_DIM
V_COLS = N_KV_HEADS * V_DIM
U_COLS = SSM_WIDTH
GA_COLS = ATTN_WIDTH
GS_COLS = SSM_WIDTH
IN_COLS = Q_COLS + K_COLS + V_COLS + U_COLS + GA_COLS + GS_COLS
SPLIT_IDX = (Q_COLS, Q_COLS + K_COLS, Q_COLS + K_COLS + V_COLS,
             Q_COLS + K_COLS + V_COLS + U_COLS, Q_COLS + K_COLS + V_COLS + U_COLS + GA_COLS)

kernel_name = 'hybrid_diffattn_s5_adaln_deepnorm_step'


def layer_norm(x, g=None, b=None):
    xf = x.astype(jnp.float32)
    mu = jnp.mean(xf, axis=-1, keepdims=True)
    var = jnp.mean(jnp.square(xf - mu), axis=-1, keepdims=True)
    y = (xf - mu) * lax.rsqrt(var + LN_EPS)
    if g is not None:
        y = y * g + b
    return y.astype(x.dtype)


def rope_partial(x, pos):
    inv = ROPE_THETA ** (-jnp.arange(0, ROT_DIM, 2, dtype=jnp.float32) / ROT_DIM)
    ang = pos.astype(jnp.float32)[:, None] * inv[None, :]
    cos = jnp.cos(ang)[None, :, None, None, :]
    sin = jnp.sin(ang)[None, :, None, None, :]
    half = ROT_DIM // 2
    x1, x2 = x[..., :half], x[..., half:ROT_DIM]
    rot = jnp.concatenate([x1 * cos - x2 * sin, x2 * cos + x1 * sin], axis=-1).astype(x.dtype)
    return jnp.concatenate([rot, x[..., ROT_DIM:]], axis=-1)


def diff_scores(q, k):
    s = jnp.einsum('bqngcd,bsncd->bngcqs', q, k)
    return s.astype(jnp.float32) * (1.0 / math.sqrt(HD))


def diff_weights(s, lam):
    p = jax.nn.softmax(s, axis=-1)
    return p[:, :, :, 0] - lam * p[:, :, :, 1]


def apply_values(a, v):
    return jnp.einsum('bngqs,bsne->bqnge', a, v.astype(jnp.float32))


def prompt_attention(q, k, v, lam):
    bsz, L = q.shape[0], q.shape[1]
    nb = L // Q_BLOCK
    qb = jnp.moveaxis(q.reshape(bsz, nb, Q_BLOCK, *q.shape[2:]), 1, 0)
    kpos = jnp.arange(L)

    def one_block(args):
        q_blk, i = args
        qpos = i * Q_BLOCK + jnp.arange(Q_BLOCK)
        s = jnp.where(qpos[:, None] >= kpos[None, :], diff_scores(q_blk, k), -jnp.inf)
        return apply_values(diff_weights(s, lam), v)

    o = lax.map(one_block, (qb, jnp.arange(nb)))
    return jnp.moveaxis(o, 0, 1).reshape(bsz, L, N_KV_HEADS, GQA_GROUP, V_DIM)


def sample_attention(q, k, v, lam, k_past, v_past):
    Ls = q.shape[1]
    P = k_past.shape[1]
    s_past = diff_scores(q, k_past)
    causal = jnp.arange(Ls)[:, None] >= jnp.arange(Ls)[None, :]
    s_new = jnp.where(causal, diff_scores(q, k), -jnp.inf)
    a = diff_weights(jnp.concatenate([s_past, s_new], axis=-1), lam)
    return apply_values(a[..., :P], v_past) + apply_values(a[..., P:], v)


def cmul(ar, ai, br, bi):
    return ar * br - ai * bi, ar * bi + ai * br


def ssm_combine(e1, e2):
    a1r, a1i, b1r, b1i = e1
    a2r, a2i, b2r, b2i = e2
    ar, ai = cmul(a2r, a2i, a1r, a1i)
    br, bi = cmul(a2r, a2i, b1r, b1i)
    return ar, ai, br + b2r, bi + b2i


def ssm_discretise(lam_re, lam_im, log_dt, b_re, b_im):
    dt = jnp.exp(log_dt.astype(jnp.float32))[:, None]
    lr, li = lam_re.astype(jnp.float32), lam_im.astype(jnp.float32)
    mag = jnp.exp(lr * dt)
    a_re, a_im = mag * jnp.cos(li * dt), mag * jnp.sin(li * dt)
    den = lr * lr + li * li
    nr, ni = a_re - 1.0, a_im
    co_re = (nr * lr + ni * li) / den
    co_im = (ni * lr - nr * li) / den
    br, bi = b_re.astype(jnp.float32), b_im.astype(jnp.float32)
    bb_re = co_re[..., None] * br - co_im[..., None] * bi
    bb_im = co_re[..., None] * bi + co_im[..., None] * br
    return a_re, a_im, bb_re, bb_im


def ssm_branch(u, h0_re, h0_im, p):
    bsz, L = u.shape[0], u.shape[1]
    a_re, a_im, bb_re, bb_im = ssm_discretise(p['ssm_lambda_re'], p['ssm_lambda_im'],
                                              p['ssm_log_dt'], p['ssm_b_re'], p['ssm_b_im'])
    ug = u.reshape(bsz, L, SSM_GROUPS, SSM_GROUP).astype(jnp.float32)
    bu_re = jnp.einsum('gpc,blgc->blgp', bb_re, ug)
    bu_im = jnp.einsum('gpc,blgc->blgp', bb_im, ug)
    chunk = SSM_CHUNK if L % SSM_CHUNK == 0 else L
    nc = L // chunk

    def to_chunks(t):
        return jnp.moveaxis(t.reshape(bsz, nc, chunk, SSM_GROUPS, SSM_STATE), 1, 0)

    c_re = p['ssm_c_re'].astype(jnp.float32)
    c_im = p['ssm_c_im'].astype(jnp.float32)

    def chunk_step(carry, xs):
        h_re, h_im = carry
        br, bi = xs
        ar = jnp.broadcast_to(a_re, br.shape)
        ai = jnp.broadcast_to(a_im, br.shape)
        cum_r, cum_i, loc_r, loc_i = lax.associative_scan(ssm_combine, (ar, ai, br, bi), axis=1)
        pr, pi = h_re[:, None], h_im[:, None]
        hr = loc_r + cum_r * pr - cum_i * pi
        hi = loc_i + cum_r * pi + cum_i * pr
        y = jnp.einsum('gcp,blgp->blgc', c_re, hr) - jnp.einsum('gcp,blgp->blgc', c_im, hi)
        return (hr[:, -1], hi[:, -1]), y

    (h_re, h_im), ys = lax.scan(chunk_step, (h0_re.astype(jnp.float32), h0_im.astype(jnp.float32)),
                                (to_chunks(bu_re), to_chunks(bu_im)))
    y = jnp.moveaxis(ys, 0, 1).reshape(bsz, L, SSM_WIDTH) + p['ssm_d'] * u
    g = jax.nn.gelu(y)
    out = g * jax.nn.sigmoid(g @ p['w_glu'] + p['b_glu'])
    return out, h_re, h_im


def trunk_layer(x, c, positions, attend, h0_re, h0_im, p, layer_idx):
    bsz, L = x.shape[0], x.shape[1]
    ada = jax.nn.silu(c) @ p['w_ada'] + p['b_ada']
    sh1, sc1, g1, sh2, sc2, g2 = jnp.split(ada[:, None, :], 6, axis=-1)
    h = layer_norm(x) * (1.0 + sc1) + sh1
    proj = h @ p['w_in']
    q, k, v, u, ga, gs = jnp.split(proj, SPLIT_IDX, axis=-1)
    q = rope_partial(q.reshape(bsz, L, N_HEADS, 2, HD), positions)
    q = q.reshape(bsz, L, N_KV_HEADS, GQA_GROUP, 2, HD)
    k = rope_partial(k.reshape(bsz, L, N_KV_HEADS, 2, HD), positions)
    v = v.reshape(bsz, L, N_KV_HEADS, V_DIM)
    lam_init = 0.8 - 0.6 * math.exp(-0.3 * layer_idx)
    lam = (jnp.exp(jnp.sum(p['lambda_q1'].astype(jnp.float32) * p['lambda_k1']))
           - jnp.exp(jnp.sum(p['lambda_q2'].astype(jnp.float32) * p['lambda_k2'])) + lam_init)
    o = attend(q, k, v, lam)
    o = o * lax.rsqrt(jnp.mean(jnp.square(o), axis=-1, keepdims=True) + LN_EPS)
    o = (o * p['subln_w'] * (1.0 - lam_init)).reshape(bsz, L, ATTN_WIDTH)
    y_ssm, h_re, h_im = ssm_branch(u, h0_re, h0_im, p)
    mixed = (jax.nn.sigmoid(ga) * o + jax.nn.sigmoid(gs) * y_ssm) @ p['w_o']
    x = layer_norm(ALPHA * x + (1.0 + g1) * mixed, p['ln1_g'], p['ln1_b'])
    h = layer_norm(x) * (1.0 + sc2) + sh2
    f = jnp.square(jax.nn.relu(h @ p['w_up'] + p['b_up'])) @ p['w_down'] + p['b_down']
    x = layer_norm(ALPHA * x + (1.0 + g2) * f, p['ln2_g'], p['ln2_b'])
    return x, k.reshape(bsz, L, N_KV_HEADS, QK_DIM), v, h_re, h_im


def setup_inputs(seed: int = 0) -> dict:
    key = jax.random.key(seed)
    ks = jax.random.split(key, 40)
    f32 = jnp.float32
    n_pages = PAST_LEN // PAGE_SIZE
    n_used = DEC_BATCH * n_pages
    n_pool = n_used + n_used // 4

    def nrm(i, shape, scale):
        return jax.random.normal(ks[i], shape, f32) * scale

    sD = D_MODEL ** -0.5
    w_in = jnp.concatenate([
        nrm(0, (DEPTH, D_MODEL, Q_COLS + K_COLS), sD),
        nrm(1, (DEPTH, D_MODEL, V_COLS), sD * BETA),
        nrm(2, (DEPTH, D_MODEL, U_COLS + GA_COLS + GS_COLS), sD)], axis=-1)
    page_table = jax.random.permutation(ks[3], n_pool)[:n_used].reshape(DEC_BATCH, n_pages).astype(jnp.int32)
    lam_re = -0.5 + nrm(4, (DEPTH, SSM_GROUPS, SSM_STATE), 0.01)
    lam_im = math.pi * jnp.arange(SSM_STATE, dtype=f32) + nrm(5, (DEPTH, SSM_GROUPS, SSM_STATE), 0.01)
    log_dt = jax.random.uniform(ks[6], (DEPTH, SSM_GROUPS), f32, math.log(1e-3), math.log(1e-1))
    return {
        'x_prompt': nrm(7, (BATCH, SEQ, D_MODEL), 1.0),
        'x_sample': nrm(8, (DEC_BATCH, DEC_SEQ, D_MODEL), 1.0),
        'cache_k': nrm(9, (DEPTH, n_pool, PAGE_SIZE, N_KV_HEADS, QK_DIM), 1.0),
        'cache_v': nrm(10, (DEPTH, n_pool, PAGE_SIZE, N_KV_HEADS, V_DIM), 1.0),
        'state_ssm_re': nrm(11, (DEPTH, DEC_BATCH, SSM_GROUPS, SSM_STATE), 0.5),
        'state_ssm_im': nrm(12, (DEPTH, DEC_BATCH, SSM_GROUPS, SSM_STATE), 0.5),
        'page_table': page_table,
        'c_prompt': nrm(13, (BATCH, D_MODEL), 1.0),
        'c_sample': nrm(14, (DEC_BATCH, D_MODEL), 1.0),
        'w_in': w_in,
        'lambda_q1': nrm(15, (DEPTH, HD), 0.1),
        'lambda_k1': nrm(16, (DEPTH, HD), 0.1),
        'lambda_q2': nrm(17, (DEPTH, HD), 0.1),
        'lambda_k2': nrm(18, (DEPTH, HD), 0.1),
        'subln_w': 1.0 + nrm(19, (DEPTH, V_DIM), 0.02),
        'ssm_lambda_re': lam_re,
        'ssm_lambda_im': lam_im,
        'ssm_log_dt': log_dt,
        'ssm_b_re': nrm(20, (DEPTH, SSM_GROUPS, SSM_STATE, SSM_GROUP), (2.0 * SSM_GROUP) ** -0.5),
        'ssm_b_im': nrm(21, (DEPTH, SSM_GROUPS, SSM_STATE, SSM_GROUP), (2.0 * SSM_GROUP) ** -0.5),
        'ssm_c_re': nrm(22, (DEPTH, SSM_GROUPS, SSM_GROUP, SSM_STATE), (2.0 * SSM_STATE) ** -0.5),
        'ssm_c_im': nrm(23, (DEPTH, SSM_GROUPS, SSM_GROUP, SSM_STATE), (2.0 * SSM_STATE) ** -0.5),
        'ssm_d': nrm(24, (DEPTH, SSM_WIDTH), 1.0),
        'w_glu': nrm(25, (DEPTH, SSM_WIDTH, SSM_WIDTH), SSM_WIDTH ** -0.5),
        'b_glu': nrm(26, (DEPTH, SSM_WIDTH), 0.02),
        'w_o': nrm(27, (DEPTH, ATTN_WIDTH, D_MODEL), ATTN_WIDTH ** -0.5 * BETA),
        'w_ada': nrm(28, (DEPTH, D_MODEL, 6 * D_MODEL), 0.02 * sD),
        'b_ada': nrm(29, (DEPTH, 6 * D_MODEL), 0.02),
        'ln1_g': 1.0 + nrm(30, (DEPTH, D_MODEL), 0.02),
        'ln1_b': nrm(31, (DEPTH, D_MODEL), 0.02),
        'ln2_g': 1.0 + nrm(32, (DEPTH, D_MODEL), 0.02),
        'ln2_b': nrm(33, (DEPTH, D_MODEL), 0.02),
        'w_up': nrm(34, (DEPTH, D_MODEL, D_FF), sD * BETA),
        'b_up': nrm(35, (DEPTH, D_FF), 0.02),
        'w_down': nrm(36, (DEPTH, D_FF, D_MODEL), D_FF ** -0.5 * BETA),
        'b_down': nrm(37, (DEPTH, D_MODEL), 0.02),
    }


def reference(x_prompt, x_sample, cache_k, cache_v, state_ssm_re, state_ssm_im, page_table,
              c_prompt, c_sample, w_in, lambda_q1, lambda_k1, lambda_q2, lambda_k2, subln_w,
              ssm_lambda_re, ssm_lambda_im, ssm_log_dt, ssm_b_re, ssm_b_im, ssm_c_re, ssm_c_im,
              ssm_d, w_glu, b_glu, w_o, w_ada, b_ada, ln1_g, ln1_b, ln2_g, ln2_b,
              w_up, b_up, w_down, b_down):
    dec_b = x_sample.shape[0]
    past_len = page_table.shape[1] * cache_k.shape[2]
    pos_prompt = jnp.arange(x_prompt.shape[1])
    pos_sample = past_len + jnp.arange(x_sample.shape[1])
    xp, xs = x_prompt, x_sample
    kp_l, vp_l, hrp_l, hip_l = [], [], [], []
    ks_l, vs_l, hrs_l, his_l = [], [], [], []
    for l in range(DEPTH):
        p = dict(w_in=w_in[l], lambda_q1=lambda_q1[l], lambda_k1=lambda_k1[l],
                 lambda_q2=lambda_q2[l], lambda_k2=lambda_k2[l], subln_w=subln_w[l],
                 ssm_lambda_re=ssm_lambda_re[l], ssm_lambda_im=ssm_lambda_im[l],
                 ssm_log_dt=ssm_log_dt[l], ssm_b_re=ssm_b_re[l], ssm_b_im=ssm_b_im[l],
                 ssm_c_re=ssm_c_re[l], ssm_c_im=ssm_c_im[l], ssm_d=ssm_d[l],
                 w_glu=w_glu[l], b_glu=b_glu[l], w_o=w_o[l], w_ada=w_ada[l], b_ada=b_ada[l],
                 ln1_g=ln1_g[l], ln1_b=ln1_b[l], ln2_g=ln2_g[l], ln2_b=ln2_b[l],
                 w_up=w_up[l], b_up=b_up[l], w_down=w_down[l], b_down=b_down[l])
        h0 = jnp.zeros((xp.shape[0], SSM_GROUPS, SSM_STATE), jnp.float32)
        xp, kp, vp, hrp, hip = trunk_layer(xp, c_prompt, pos_prompt, prompt_attention, h0, h0, p, l)
        k_past = cache_k[l, page_table].reshape(dec_b, past_len, N_KV_HEADS, 2, HD)
        v_past = cache_v[l, page_table].reshape(dec_b, past_len, N_KV_HEADS, V_DIM)
        attend_s = functools.partial(sample_attention, k_past=k_past, v_past=v_past)
        xs, kn, vn, hrs, his = trunk_layer(xs, c_sample, pos_sample, attend_s,
                                           state_ssm_re[l], state_ssm_im[l], p, l)
        kp_l.append(kp); vp_l.append(vp); hrp_l.append(hrp); hip_l.append(hip)
        ks_l.append(kn); vs_l.append(vn); hrs_l.append(hrs); his_l.append(his)
    return (xp, xs,
            jnp.stack(kp_l), jnp.stack(vp_l), jnp.stack(hrp_l), jnp.stack(hip_l),
            jnp.stack(ks_l), jnp.stack(vs_l), jnp.stack(hrs_l), jnp.stack(his_l))
```

```python
import functools
import math

import jax
import jax.numpy as jnp
from jax import lax
from jax.experimental import pallas as pl
from jax.experimental.pallas import tpu as pltpu

F32 = jnp.float32
BF16 = jnp.bfloat16

D_MODEL = 1024
N_HEADS = 8
N_KV_HEADS = 4
GQA_GROUP = N_HEADS // N_KV_HEADS
HD = 64
QK_DIM = 2 * HD
V_DIM = 2 * HD
ROT_DIM = HD // 4
ROPE_THETA = 500000.0
SSM_GROUP = 16
SSM_GROUPS = D_MODEL // SSM_GROUP
SSM_STATE = 64
SSM_STATES = SSM_GROUPS * SSM_STATE
D_FF = 4 * D_MODEL
DEPTH = 1
ALPHA = (2.0 * DEPTH) ** 0.25
LN_EPS = 1e-5
LAM_INIT = 0.8 - 0.6 * math.exp(-0.3 * 0)
Q_COLS = N_HEADS * QK_DIM
K_COLS = N_KV_HEADS * QK_DIM
V_COLS = N_KV_HEADS * V_DIM
IN_COLS = Q_COLS + K_COLS + V_COLS + 3 * D_MODEL

LANES = 128
SSM_CHUNK = 16
CHUNK_COLS = SSM_CHUNK * SSM_GROUP
PAIR_STATES = 2 * SSM_STATE
VMEM_LIMIT = 48 * 1024 * 1024

_NT = (((1,), (1,)), ((), ()))


def _cparams(n_axes):
    return pltpu.CompilerParams(dimension_semantics=("arbitrary",) * n_axes, vmem_limit_bytes=VMEM_LIMIT)


def _const_spec(shape):
    nd = len(shape)
    return pl.BlockSpec(shape, lambda *_: (0,) * nd, pipeline_mode=pl.Buffered(1))


def _layer_norm(x):
    mu = jnp.mean(x, axis=-1, keepdims=True)
    xc = x - mu
    var = jnp.mean(xc * xc, axis=-1, keepdims=True)
    return xc * lax.rsqrt(var + LN_EPS)


def _ada_kernel(c_ref, w_ref, b_ref, o_ref):
    c = c_ref[...]
    s = c * jax.nn.sigmoid(c)
    o_ref[...] = jnp.dot(s.astype(BF16), w_ref[...], preferred_element_type=F32) + b_ref[...]


def _ada(c_all, w_ada, b_ada):
    rows = c_all.shape[0]
    tn = D_MODEL
    return pl.pallas_call(
        _ada_kernel,
        grid=(6 * D_MODEL // tn,),
        in_specs=[pl.BlockSpec((rows, D_MODEL), lambda j: (0, 0)),
                  pl.BlockSpec((D_MODEL, tn), lambda j: (0, j)),
                  pl.BlockSpec((1, tn), lambda j: (0, j))],
        out_specs=pl.BlockSpec((rows, tn), lambda j: (0, j)),
        out_shape=jax.ShapeDtypeStruct((rows, 6 * D_MODEL), F32),
        compiler_params=_cparams(1),
    )(c_all, w_ada, b_ada)


def _proj_kernel(x_ref, sc_ref, sh_ref, cos_ref, sa_ref, sb_ref, w_ref,
                 q_ref, k_ref, v_ref, kb_ref, vb_ref, u_ref, ga_ref, gs_ref):
    h = (_layer_norm(x_ref[...]) * (1.0 + sc_ref[0]) + sh_ref[0]).astype(BF16)
    cos_t, sin_a, sin_b = cos_ref[...], sa_ref[...], sb_ref[...]

    def rope(t):
        up = pltpu.roll(t, LANES - ROT_DIM // 2, 1)
        dn = pltpu.roll(t, ROT_DIM // 2, 1)
        return t * cos_t + up * sin_a + dn * sin_b

    def mm(c0, c1):
        return jnp.dot(h, w_ref[:, c0:c1], preferred_element_type=F32)

    q = mm(0, Q_COLS)
    for s in range(Q_COLS // LANES):
        sl = slice(s * LANES, (s + 1) * LANES)
        q_ref[:, sl] = (rope(q[:, sl]) * (1.0 / math.sqrt(HD))).astype(BF16)
    k = mm(Q_COLS, Q_COLS + K_COLS)
    for s in range(K_COLS // LANES):
        sl = slice(s * LANES, (s + 1) * LANES)
        kr = rope(k[:, sl])
        k_ref[:, sl] = kr
        kb_ref[:, sl] = kr.astype(BF16)
    c0 = Q_COLS + K_COLS
    v = mm(c0, c0 + V_COLS)
    v_ref[...] = v
    vb_ref[...] = v.astype(BF16)
    c0 += V_COLS
    u_ref[...] = mm(c0, c0 + D_MODEL)
    c0 += D_MODEL
    ga_ref[...] = mm(c0, c0 + D_MODEL).astype(BF16)
    c0 += D_MODEL
    gs_ref[...] = mm(c0, c0 + D_MODEL).astype(BF16)


def _proj(x, sc, sh, tables, w_in, *, tm, rows_per_mod, table_blocks):
    T = x.shape[0]
    mod_rows = sc.shape[1]
    mod_spec = pl.BlockSpec((1, mod_rows, D_MODEL), lambda i: (i * tm // rows_per_mod, 0, 0))
    tab_spec = pl.BlockSpec((tm, LANES), lambda i: (i % table_blocks, 0))
    row = lambda w: pl.BlockSpec((tm, w), lambda i: (i, 0))
    outs = [(Q_COLS, BF16), (K_COLS, F32), (V_COLS, F32), (K_COLS, BF16), (V_COLS, BF16),
            (D_MODEL, F32), (D_MODEL, BF16), (D_MODEL, BF16)]
    return pl.pallas_call(
        _proj_kernel,
        grid=(T // tm,),
        in_specs=[row(D_MODEL), mod_spec, mod_spec, tab_spec, tab_spec, tab_spec,
                  _const_spec((D_MODEL, IN_COLS))],
        out_specs=[row(w) for w, _ in outs],
        out_shape=[jax.ShapeDtypeStruct((T, w), dt) for w, dt in outs],
        compiler_params=_cparams(1),
    )(x, sc, sh, *tables, w_in)


def _rope_tables(pos):
    inv = ROPE_THETA ** (-jnp.arange(0, ROT_DIM, 2, dtype=F32) / ROT_DIM)
    ang = pos.astype(F32)[:, None] * inv[None, :]
    cos, sin = jnp.cos(ang), jnp.sin(ang)
    n, half = pos.shape[0], ROT_DIM // 2
    ones = jnp.ones((n, HD - ROT_DIM), F32)
    zeros = lambda w: jnp.zeros((n, w), F32)
    cos_t = jnp.concatenate([cos, cos, ones], axis=-1)
    sin_a = jnp.concatenate([-sin, zeros(HD - half)], axis=-1)
    sin_b = jnp.concatenate([zeros(half), sin, zeros(HD - ROT_DIM)], axis=-1)
    return tuple(jnp.tile(t, (1, LANES // HD)) for t in (cos_t, sin_a, sin_b))


def _lambda(lq1, lk1, lq2, lk2):
    return (jnp.exp(jnp.sum(lq1[...] * lk1[...], axis=-1, keepdims=True))
            - jnp.exp(jnp.sum(lq2[...] * lk2[...], axis=-1, keepdims=True)) + LAM_INIT)


def _sub_norm(o, subw):
    o = o * lax.rsqrt(jnp.mean(o * o, axis=-1, keepdims=True) + LN_EPS)
    return o * subw * (1.0 - LAM_INIT)


def _attn_kernel(lq1, lk1, lq2, lk2, subw_ref, q_ref, k_ref, v_ref, o_ref, q4_s, m_s, l_s, acc_s, *, tq):
    qi = pl.program_id(2)
    comp0 = lax.broadcasted_iota(jnp.int32, (1, QK_DIM), 1) < HD
    for g in range(GQA_GROUP):
        qg = q_ref[:, g * QK_DIM:(g + 1) * QK_DIM].astype(F32)
        q4_s[2 * g] = jnp.where(comp0, qg, 0.0).astype(BF16)
        q4_s[2 * g + 1] = jnp.where(comp0, 0.0, qg).astype(BF16)
    m_s[...] = jnp.full(m_s.shape, -jnp.inf, F32)
    l_s[...] = jnp.zeros(l_s.shape, F32)
    acc_s[...] = jnp.zeros(acc_s.shape, F32)

    def step(ki, masked):
        start = pl.multiple_of(ki * tq, tq)
        kblk = k_ref[pl.ds(start, tq), :]
        vblk = v_ref[pl.ds(start, tq), :]
        if masked:
            keep = (lax.broadcasted_iota(jnp.int32, (tq, tq), 0) >= lax.broadcasted_iota(jnp.int32, (tq, tq), 1))
        for mp in range(2 * GQA_GROUP):
            s = lax.dot_general(q4_s[mp], kblk, _NT, preferred_element_type=F32)
            if masked:
                s = jnp.where(keep, s, -jnp.inf)
            m_prev = m_s[mp]
            m_new = jnp.maximum(m_prev, jnp.max(s, axis=-1, keepdims=True))
            alpha = jnp.exp(m_prev - m_new)
            p = jnp.exp(s - m_new)
            l_s[mp] = alpha * l_s[mp] + jnp.sum(p, axis=-1, keepdims=True)
            acc_s[mp] = alpha * acc_s[mp] + jnp.dot(p.astype(BF16), vblk, preferred_element_type=F32)
            m_s[mp] = m_new

    def body(ki, carry):
        step(ki, False)
        return carry

    lax.fori_loop(0, qi, body, 0)
    step(qi, True)

    lam = _lambda(lq1, lk1, lq2, lk2)
    for g in range(GQA_GROUP):
        o = acc_s[2 * g] / l_s[2 * g] - lam * (acc_s[2 * g + 1] / l_s[2 * g + 1])
        o_ref[:, g * V_DIM:(g + 1) * V_DIM] = _sub_norm(o, subw_ref[...]).astype(BF16)


def _prompt_attention(q, kb, vb, lams, subw, *, batch, seq, tq):
    T = batch * seq
    nq = seq // tq
    n_maps = 2 * GQA_GROUP
    small = pl.BlockSpec((1, HD), lambda b, n, i: (0, 0))
    return pl.pallas_call(
        functools.partial(_attn_kernel, tq=tq),
        grid=(batch, N_KV_HEADS, nq),
        in_specs=[small, small, small, small,
                  pl.BlockSpec((1, V_DIM), lambda b, n, i: (0, 0)),
                  pl.BlockSpec((tq, GQA_GROUP * QK_DIM), lambda b, n, i: (b * nq + i, n)),
                  pl.BlockSpec((seq, QK_DIM), lambda b, n, i: (b, n)),
                  pl.BlockSpec((seq, V_DIM), lambda b, n, i: (b, n))],
        out_specs=pl.BlockSpec((tq, GQA_GROUP * V_DIM), lambda b, n, i: (b * nq + i, n)),
        out_shape=jax.ShapeDtypeStruct((T, N_HEADS * V_DIM), BF16),
        scratch_shapes=[pltpu.VMEM((n_maps, tq, QK_DIM), BF16),
                        pltpu.VMEM((n_maps, tq, 1), F32),
                        pltpu.VMEM((n_maps, tq, 1), F32),
                        pltpu.VMEM((n_maps, tq, V_DIM), F32)],
        compiler_params=_cparams(3),
    )(*lams, subw, q, kb, vb)


DEC_PAGES_PER_STEP = 8
DEC_ROWS = 2 * N_HEADS
KV_ROW = N_KV_HEADS * QK_DIM


def _decode_kernel(pt_ref, lq1, lk1, lq2, lk2, subw_ref, qbd_ref, knew_ref, vnew_ref, *rest, page):
    del pt_ref
    npg = DEC_PAGES_PER_STEP
    k_refs, v_refs = rest[:npg], rest[npg:2 * npg]
    o_ref, m_s, l_s, acc_s = rest[2 * npg:]
    c = pl.program_id(1)
    qb = qbd_ref[0]

    @pl.when(c == 0)
    def _():
        m_s[...] = jnp.sum(qb.astype(F32) * knew_ref[0], axis=-1, keepdims=True)
        l_s[...] = jnp.ones(l_s.shape, F32)
        acc_s[...] = jnp.broadcast_to(vnew_ref[0], acc_s.shape)

    s = jnp.concatenate(
        [lax.dot_general(qb, k_refs[j][0].astype(BF16), _NT, preferred_element_type=F32) for j in range(npg)],
        axis=-1)
    m_prev = m_s[...]
    m_new = jnp.maximum(m_prev, jnp.max(s, axis=-1, keepdims=True))
    alpha = jnp.exp(m_prev - m_new)
    p = jnp.exp(s - m_new)
    l_s[...] = alpha * l_s[...] + jnp.sum(p, axis=-1, keepdims=True)
    pv = jnp.dot(p[:, :page].astype(BF16), v_refs[0][0].astype(BF16), preferred_element_type=F32)
    for j in range(1, npg):
        pv += jnp.dot(p[:, j * page:(j + 1) * page].astype(BF16), v_refs[j][0].astype(BF16),
                      preferred_element_type=F32)
    acc_s[...] = alpha * acc_s[...] + pv
    m_s[...] = m_new

    @pl.when(c == pl.num_programs(1) - 1)
    def _():
        lam = _lambda(lq1, lk1, lq2, lk2)
        o = acc_s[...] / l_s[...]
        d = o[:N_HEADS] - lam * o[N_HEADS:]
        row_head = lax.broadcasted_iota(jnp.int32, (N_HEADS, V_DIM), 0) // GQA_GROUP
        sel = jnp.zeros((N_HEADS, V_DIM), F32)
        for n in range(N_KV_HEADS):
            sel += jnp.where(row_head == n, d[:, n * V_DIM:(n + 1) * V_DIM], 0.0)
        o_ref[0] = _sub_norm(sel, subw_ref[...])


def _decode_attention(q, k_new, v_new, cache_k, cache_v, page_table, lams, subw):
    bsz = q.shape[0]
    page = cache_k.shape[1]
    n_pages = page_table.shape[1]
    npg = DEC_PAGES_PER_STEP
    q5 = q.reshape(bsz, N_KV_HEADS, GQA_GROUP, 2, HD)
    qbd = jnp.einsum('bngcd,nm,ce->bcngmed', q5, jnp.eye(N_KV_HEADS, dtype=q.dtype), jnp.eye(2, dtype=q.dtype))
    qbd = qbd.reshape(bsz, DEC_ROWS, KV_ROW)
    small = pl.BlockSpec((1, HD), lambda b, c, pt: (0, 0))
    page_specs = [pl.BlockSpec((1, page, KV_ROW), functools.partial(lambda b, c, pt, j: (pt[b, c * npg + j], 0, 0), j=j))
                  for j in range(npg)]
    grid_spec = pltpu.PrefetchScalarGridSpec(
        num_scalar_prefetch=1,
        grid=(bsz, n_pages // npg),
        in_specs=[small, small, small, small,
                  pl.BlockSpec((1, V_DIM), lambda b, c, pt: (0, 0)),
                  pl.BlockSpec((1, DEC_ROWS, KV_ROW), lambda b, c, pt: (b, 0, 0)),
                  pl.BlockSpec((1, 1, KV_ROW), lambda b, c, pt: (b, 0, 0)),
                  pl.BlockSpec((1, 1, KV_ROW), lambda b, c, pt: (b, 0, 0))] + page_specs + page_specs,
        out_specs=pl.BlockSpec((1, N_HEADS, V_DIM), lambda b, c, pt: (b, 0, 0)),
        scratch_shapes=[pltpu.VMEM((DEC_ROWS, 1), F32), pltpu.VMEM((DEC_ROWS, 1), F32),
                        pltpu.VMEM((DEC_ROWS, KV_ROW), F32)])
    o = pl.pallas_call(
        functools.partial(_decode_kernel, page=page),
        grid_spec=grid_spec,
        out_shape=jax.ShapeDtypeStruct((bsz, N_HEADS, V_DIM), F32),
        compiler_params=_cparams(2),
    )(page_table, *lams, subw, qbd, k_new.reshape(bsz, 1, KV_ROW), v_new.reshape(bsz, 1, KV_ROW),
      *([cache_k] * npg), *([cache_v] * npg))
    return o.reshape(bsz, N_HEADS * V_DIM)


N_POW = 24


def _ssm_prep_kernel(lr_ref, li_ref, ldt_ref, cre_ref, cim_ref, bre_ref, bim_ref,
                     kern_ref, wsr_ref, wsi_ref, gre_ref, gim_ref, a1r_ref, a1i_ref, acr_ref, aci_ref,
                     bbr_ref, bbi_ref):
    lr, li = lr_ref[0], li_ref[0]
    dt = jnp.exp(ldt_ref[0])
    ell = lax.broadcasted_iota(jnp.int32, (N_POW, 1), 0).astype(F32)
    mag = jnp.exp(ell * (lr * dt))
    ph = ell * (li * dt)
    p_re, p_im = mag * jnp.cos(ph), mag * jnp.sin(ph)
    a_re, a_im = p_re[1:2], p_im[1:2]
    a1r_ref[0], a1i_ref[0] = a_re, a_im
    acr_ref[0], aci_ref[0] = p_re[SSM_CHUNK:SSM_CHUNK + 1], p_im[SSM_CHUNK:SSM_CHUNK + 1]
    den = lr * lr + li * li
    nr, ni = a_re - 1.0, a_im
    co_re = (nr * lr + ni * li) / den
    co_im = (ni * lr - nr * li) / den
    b_re, b_im = bre_ref[0], bim_ref[0]
    bb_re = co_re * b_re - co_im * b_im
    bb_im = co_re * b_im + co_im * b_re
    bbr_ref[0], bbi_ref[0] = bb_re, bb_im
    c_re, c_im = cre_ref[0], cim_ref[0]
    for l in range(SSM_CHUNK + 1):
        rows = slice(l * SSM_GROUP, (l + 1) * SSM_GROUP)
        gre_ref[0, rows, :] = c_re * p_re[l:l + 1] - c_im * p_im[l:l + 1]
        gim_ref[0, rows, :] = c_re * p_im[l:l + 1] + c_im * p_re[l:l + 1]
    for s in range(SSM_CHUNK):
        rows = slice(s * SSM_GROUP, (s + 1) * SSM_GROUP)
        e = SSM_CHUNK - 1 - s
        wsr_ref[0, rows, :] = p_re[e:e + 1] * bb_re - p_im[e:e + 1] * bb_im
        wsi_ref[0, rows, :] = p_re[e:e + 1] * bb_im + p_im[e:e + 1] * bb_re
    g_re = gre_ref[0, :CHUNK_COLS, :]
    g_im = gim_ref[0, :CHUNK_COLS, :]
    hi = lax.Precision.HIGHEST
    kern_ref[0] = (lax.dot_general(g_re, bb_re, _NT, precision=hi, preferred_element_type=F32)
                   - lax.dot_general(g_im, bb_im, _NT, precision=hi, preferred_element_type=F32))


def _ssm_prep(lam_re, lam_im, log_dt, b_re, b_im, c_re, c_im):
    G, P, C = SSM_GROUPS, SSM_STATE, SSM_GROUP
    g_rows = (SSM_CHUNK + 1) * C
    vec = pl.BlockSpec((1, 1, P), lambda g: (g, 0, 0))
    mat = pl.BlockSpec((1, C, P), lambda g: (g, 0, 0))
    out_defs = [((CHUNK_COLS, C), None), ((CHUNK_COLS, P), None), ((CHUNK_COLS, P), None),
                ((g_rows, P), None), ((g_rows, P), None),
                ((1, P), None), ((1, P), None), ((1, P), None), ((1, P), None), ((C, P), None), ((C, P), None)]
    return pl.pallas_call(
        _ssm_prep_kernel,
        grid=(G,),
        in_specs=[vec, vec, pl.BlockSpec((1, 1, 1), lambda g: (g, 0, 0)), mat, mat, mat, mat],
        out_specs=[pl.BlockSpec((1,) + s, lambda g: (g, 0, 0)) for s, _ in out_defs],
        out_shape=[jax.ShapeDtypeStruct((G,) + s, F32) for s, _ in out_defs],
        compiler_params=_cparams(1),
    )(lam_re.reshape(G, 1, P), lam_im.reshape(G, 1, P), log_dt.reshape(G, 1, 1),
      c_re, c_im, jnp.swapaxes(b_re, 1, 2), jnp.swapaxes(b_im, 1, 2))


def _pair_block_diag(w):
    g, a, b = w.shape
    w2 = w.reshape(g // 2, 2, a, b)
    return jnp.einsum('qjab,jk->qjakb', w2, jnp.eye(2, dtype=w.dtype)).reshape(g // 2, 2 * a, 2 * b)


def _tile_block_diag(w, per_tile):
    g, a, b = w.shape
    w2 = w.reshape(g // per_tile, per_tile, a, b)
    eye = jnp.eye(per_tile, dtype=w.dtype)
    return jnp.einsum('qjab,jk->qjakb', w2, eye).reshape(g // per_tile, per_tile * a, per_tile * b)


def _ssm_state_kernel(uc_ref, wsr_ref, wsi_ref, sre_ref, sim_ref):
    def both(w_ref):
        return jnp.concatenate([jnp.dot(uc_ref[j], w_ref[j], preferred_element_type=F32) for j in range(2)], axis=-1)
    sre_ref[...] = both(wsr_ref)
    sim_ref[...] = both(wsi_ref)


def _ssm_scan_kernel(ar_ref, ai_ref, sre_ref, sim_ref, hpr_ref, hpi_ref, hfr_ref, hfi_ref):
    ar, ai = ar_ref[...], ai_ref[...]
    n_chunks = sre_ref.shape[0]

    def body(k, carry):
        hr, hi = carry
        hpr_ref[k] = hr
        hpi_ref[k] = hi
        return (ar * hr - ai * hi + sre_ref[k], ar * hi + ai * hr + sim_ref[k])

    zero = jnp.zeros(hfr_ref.shape, F32)
    hr, hi = lax.fori_loop(0, n_chunks, body, (zero, zero))
    hfr_ref[...] = hr
    hfi_ref[...] = hi


def _ssm_out_kernel(uc_ref, toep_ref, hpr_ref, hpi_ref, wcr_ref, wci_ref, y_ref):
    y = jnp.concatenate([jnp.dot(uc_ref[j], toep_ref[j], preferred_element_type=F32) for j in range(2)], axis=-1)
    y += jnp.dot(hpr_ref[...].astype(BF16), wcr_ref[0], preferred_element_type=F32)
    y += jnp.dot(hpi_ref[...].astype(BF16), wci_ref[0], preferred_element_type=F32)
    y_ref[0] = y.astype(BF16)


def _prompt_ssm(u, prep, *, batch, seq):
    kern, wsr, wsi, g_re, g_im, _, _, ac_re, ac_im, _, _ = prep
    G, P, C, TL = SSM_GROUPS, SSM_STATE, SSM_GROUP, SSM_CHUNK
    nk = seq // TL
    rows = nk * batch
    uc = u.reshape(batch, nk, TL, G, C).transpose(3, 1, 0, 2, 4).reshape(G, rows, CHUNK_COLS).astype(BF16)
    k4 = kern.reshape(G, TL, C, C)
    lag = jnp.arange(TL)[None, :] - jnp.arange(TL)[:, None]
    toep = jnp.where((lag >= 0)[None, :, :, None, None], k4[:, jnp.clip(lag, 0, TL - 1)], 0.0)
    toep = toep.transpose(0, 1, 4, 2, 3).reshape(G, CHUNK_COLS, CHUNK_COLS).astype(BF16)
    wc_re = g_re.reshape(G, TL + 1, C, P)[:, 1:].transpose(0, 3, 1, 2).reshape(G, P, CHUNK_COLS)
    wc_im = -g_im.reshape(G, TL + 1, C, P)[:, 1:].transpose(0, 3, 1, 2).reshape(G, P, CHUNK_COLS)
    wc_re, wc_im = _pair_block_diag(wc_re).astype(BF16), _pair_block_diag(wc_im).astype(BF16)

    pair3 = lambda a, b: pl.BlockSpec((2, a, b), lambda q: (q, 0, 0))
    s_spec = pl.BlockSpec((rows, PAIR_STATES), lambda q: (0, q))
    s_re, s_im = pl.pallas_call(
        _ssm_state_kernel,
        grid=(G // 2,),
        in_specs=[pair3(rows, CHUNK_COLS), pair3(CHUNK_COLS, P), pair3(CHUNK_COLS, P)],
        out_specs=[s_spec, s_spec],
        out_shape=[jax.ShapeDtypeStruct((rows, G * P), F32)] * 2,
        compiler_params=_cparams(1),
    )(uc, wsr.astype(BF16), wsi.astype(BF16))

    wb = 512
    a_spec = pl.BlockSpec((1, wb), lambda j: (0, j))
    seq_spec = pl.BlockSpec((nk, batch, wb), lambda j: (0, 0, j))
    fin_spec = pl.BlockSpec((batch, wb), lambda j: (0, j))
    hp_re, hp_im, hf_re, hf_im = pl.pallas_call(
        _ssm_scan_kernel,
        grid=(G * P // wb,),
        in_specs=[a_spec, a_spec, seq_spec, seq_spec],
        out_specs=[seq_spec, seq_spec, fin_spec, fin_spec],
        out_shape=[jax.ShapeDtypeStruct((nk, batch, G * P), F32)] * 2 + [jax.ShapeDtypeStruct((batch, G * P), F32)] * 2,
        compiler_params=_cparams(1),
    )(ac_re.reshape(1, G * P), ac_im.reshape(1, G * P),
      s_re.reshape(nk, batch, G * P), s_im.reshape(nk, batch, G * P))

    yc = pl.pallas_call(
        _ssm_out_kernel,
        grid=(G // 2,),
        in_specs=[pair3(rows, CHUNK_COLS), pair3(CHUNK_COLS, CHUNK_COLS), s_spec, s_spec,
                  pl.BlockSpec((1, PAIR_STATES, 2 * CHUNK_COLS), lambda q: (q, 0, 0)),
                  pl.BlockSpec((1, PAIR_STATES, 2 * CHUNK_COLS), lambda q: (q, 0, 0))],
        out_specs=pl.BlockSpec((1, rows, 2 * CHUNK_COLS), lambda q: (q, 0, 0)),
        out_shape=jax.ShapeDtypeStruct((G // 2, rows, 2 * CHUNK_COLS), BF16),
        compiler_params=_cparams(1),
    )(uc, toep, hp_re.reshape(rows, G * P), hp_im.reshape(rows, G * P), wc_re, wc_im)
    y = yc.reshape(G // 2, nk, batch, 2, TL, C).transpose(2, 1, 4, 0, 3, 5).reshape(batch * seq, G * C)
    return y, hf_re, hf_im


SSM_TILE_GROUPS = 16


def _ssm_step_kernel(u_ref, h0r_ref, h0i_ref, ar_ref, ai_ref, bdr_ref, bdi_ref, cdr_ref, cdi_ref,
                     hr_ref, hi_ref, y_ref):
    ch = SSM_TILE_GROUPS * SSM_GROUP
    st = SSM_TILE_GROUPS * SSM_STATE
    for t in range(SSM_GROUPS // SSM_TILE_GROUPS):
        cs, ss = slice(t * ch, (t + 1) * ch), slice(t * st, (t + 1) * st)
        ub = u_ref[:, cs].astype(BF16)
        ar, ai, h0r, h0i = ar_ref[:, ss], ai_ref[:, ss], h0r_ref[:, ss], h0i_ref[:, ss]
        hr = ar * h0r - ai * h0i + jnp.dot(ub, bdr_ref[t], preferred_element_type=F32)
        hi = ar * h0i + ai * h0r + jnp.dot(ub, bdi_ref[t], preferred_element_type=F32)
        hr_ref[:, ss] = hr
        hi_ref[:, ss] = hi
        y_ref[:, cs] = (jnp.dot(hr.astype(BF16), cdr_ref[t], preferred_element_type=F32)
                        + jnp.dot(hi.astype(BF16), cdi_ref[t], preferred_element_type=F32))


def _sample_ssm(u, h0_re, h0_im, prep, c_re, c_im):
    _, _, _, _, _, a_re, a_im, _, _, bb_re, bb_im = prep
    bsz = u.shape[0]
    n_st = SSM_STATES
    bd_re = _tile_block_diag(bb_re, SSM_TILE_GROUPS).astype(BF16)
    bd_im = _tile_block_diag(bb_im, SSM_TILE_GROUPS).astype(BF16)
    cd_re = _tile_block_diag(jnp.swapaxes(c_re, 1, 2), SSM_TILE_GROUPS).astype(BF16)
    cd_im = _tile_block_diag(-jnp.swapaxes(c_im, 1, 2), SSM_TILE_GROUPS).astype(BF16)
    args = (u, h0_re.reshape(bsz, n_st), h0_im.reshape(bsz, n_st), a_re.reshape(1, n_st), a_im.reshape(1, n_st),
            bd_re, bd_im, cd_re, cd_im)
    return pl.pallas_call(
        _ssm_step_kernel,
        grid=(1,),
        in_specs=[_const_spec(a.shape) for a in args],
        out_specs=[pl.BlockSpec((bsz, n_st), lambda i: (0, 0))] * 2 + [pl.BlockSpec((bsz, D_MODEL), lambda i: (0, 0))],
        out_shape=[jax.ShapeDtypeStruct((bsz, n_st), F32)] * 2 + [jax.ShapeDtypeStruct((bsz, D_MODEL), F32)],
        compiler_params=_cparams(1),
    )(*args)


FF_CHUNK = 1024


def _merge_kernel(x_ref, o_ref, ga_ref, gs_ref, y_ref, u_ref, g1_ref, sc2_ref, sh2_ref, g2_ref,
                  d_ref, wglu_ref, bglu_ref, wo_ref, ln1g_ref, ln1b_ref,
                  wup_ref, bup_ref, wdn_ref, bdn_ref, ln2g_ref, ln2b_ref, out_ref):
    yd = y_ref[...].astype(F32) + d_ref[...] * u_ref[...]
    g = jax.nn.gelu(yd, approximate=True)
    z = jnp.dot(g.astype(BF16), wglu_ref[...], preferred_element_type=F32) + bglu_ref[...]
    y_ssm = g * jax.nn.sigmoid(z)
    mixed = (jax.nn.sigmoid(ga_ref[...].astype(F32)) * o_ref[...].astype(F32)
             + jax.nn.sigmoid(gs_ref[...].astype(F32)) * y_ssm)
    t = jnp.dot(mixed.astype(BF16), wo_ref[...], preferred_element_type=F32)
    x1 = _layer_norm(ALPHA * x_ref[...] + (1.0 + g1_ref[0]) * t) * ln1g_ref[...] + ln1b_ref[...]
    h2 = (_layer_norm(x1) * (1.0 + sc2_ref[0]) + sh2_ref[0]).astype(BF16)
    f = jnp.zeros(x1.shape, F32)
    for c in range(D_FF // FF_CHUNK):
        cs = slice(c * FF_CHUNK, (c + 1) * FF_CHUNK)
        a = jnp.maximum(jnp.dot(h2, wup_ref[:, cs], preferred_element_type=F32) + bup_ref[:, cs], 0.0)
        f += jnp.dot((a * a).astype(BF16), wdn_ref[cs, :], preferred_element_type=F32)
    f += bdn_ref[...]
    out_ref[...] = _layer_norm(ALPHA * x1 + (1.0 + g2_ref[0]) * f) * ln2g_ref[...] + ln2b_ref[...]


def _merge(x, o, ga, gs, y, u, mods, params, *, tm, rows_per_mod):
    T = x.shape[0]
    mod_rows = mods[0].shape[1]
    mod_spec = pl.BlockSpec((1, mod_rows, D_MODEL), lambda i: (i * tm // rows_per_mod, 0, 0))
    row = pl.BlockSpec((tm, D_MODEL), lambda i: (i, 0))
    return pl.pallas_call(
        _merge_kernel,
        grid=(T // tm,),
        in_specs=[row] * 6 + [mod_spec] * 4 + [_const_spec(p.shape) for p in params],
        out_specs=row,
        out_shape=jax.ShapeDtypeStruct((T, D_MODEL), F32),
        compiler_params=_cparams(1),
    )(x, o, ga, gs, y, u, *mods, *params)


def _split_mods(ada):
    return [ada[..., i * D_MODEL:(i + 1) * D_MODEL] for i in range(6)]


def kernel(x_prompt, x_sample, cache_k, cache_v, state_ssm_re, state_ssm_im, page_table, c_prompt, c_sample, w_in, lambda_q1, lambda_k1, lambda_q2, lambda_k2, subln_w, ssm_lambda_re, ssm_lambda_im, ssm_log_dt, ssm_b_re, ssm_b_im, ssm_c_re, ssm_c_im, ssm_d, w_glu, b_glu, w_o, w_ada, b_ada, ln1_g, ln1_b, ln2_g, ln2_b, w_up, b_up, w_down, b_down):
    assert w_in.shape[0] == DEPTH and x_sample.shape[1] == 1
    batch, seq, _ = x_prompt.shape
    dec_b = x_sample.shape[0]
    page = cache_k.shape[2]
    past_len = page_table.shape[1] * page
    l = 0
    row2 = lambda a: a[l].reshape(1, -1)

    pad = (-batch) % 8
    c_all = jnp.concatenate([c_prompt, jnp.zeros((pad, D_MODEL), F32), c_sample], axis=0)
    ada = _ada(c_all, w_ada[l].astype(BF16), row2(b_ada))
    mods_p = _split_mods(ada[:batch].reshape(batch, 1, 6 * D_MODEL))
    mods_s = _split_mods(ada[batch + pad:].reshape(1, dec_b, 6 * D_MODEL))

    w_in_b = w_in[l].astype(BF16)
    lams = (row2(lambda_q1), row2(lambda_k1), row2(lambda_q2), row2(lambda_k2))
    subw = row2(subln_w)
    merge_params = (row2(ssm_d), w_glu[l].astype(BF16), row2(b_glu), w_o[l].astype(BF16), row2(ln1_g), row2(ln1_b),
                    w_up[l].astype(BF16), row2(b_up), w_down[l].astype(BF16), row2(b_down), row2(ln2_g), row2(ln2_b))
    prep = _ssm_prep(ssm_lambda_re[l], ssm_lambda_im[l], ssm_log_dt[l], ssm_b_re[l], ssm_b_im[l],
                     ssm_c_re[l], ssm_c_im[l])

    tm = 512
    T = batch * seq
    xp = x_prompt.reshape(T, D_MODEL)
    q, k, v, kb, vb, u, ga, gs = _proj(xp, mods_p[1], mods_p[0], _rope_tables(jnp.arange(seq)), w_in_b,
                                       tm=tm, rows_per_mod=seq, table_blocks=seq // tm)
    o = _prompt_attention(q, kb, vb, lams, subw, batch=batch, seq=seq, tq=512)
    y, hf_re, hf_im = _prompt_ssm(u, prep, batch=batch, seq=seq)
    yp = _merge(xp, o, ga, gs, y, u, (mods_p[2], mods_p[4], mods_p[3], mods_p[5]), merge_params,
                tm=256, rows_per_mod=seq)

    xs = x_sample.reshape(dec_b, D_MODEL)
    pos_s = jnp.full((dec_b,), past_len, jnp.int32)
    qs, ks, vs, _, _, us, gas, gss = _proj(xs, mods_s[1], mods_s[0], _rope_tables(pos_s), w_in_b,
                                           tm=dec_b, rows_per_mod=dec_b, table_blocks=1)
    os_ = _decode_attention(qs, ks, vs, cache_k[l].reshape(-1, page, KV_ROW), cache_v[l].reshape(-1, page, KV_ROW),
                            page_table, lams, subw).astype(BF16)
    hs_re, hs_im, y_s = _sample_ssm(us, state_ssm_re[l], state_ssm_im[l], prep, ssm_c_re[l], ssm_c_im[l])
    ys = _merge(xs, os_, gas, gss, y_s, us, (mods_s[2], mods_s[4], mods_s[3], mods_s[5]), merge_params,
                tm=dec_b, rows_per_mod=dec_b)

    st = (SSM_GROUPS, SSM_STATE)
    return (yp.reshape(batch, seq, D_MODEL), ys.reshape(dec_b, 1, D_MODEL),
            k.reshape(1, batch, seq, N_KV_HEADS, QK_DIM), v.reshape(1, batch, seq, N_KV_HEADS, V_DIM),
            hf_re.reshape(1, batch, *st), hf_im.reshape(1, batch, *st),
            ks.reshape(1, dec_b, 1, N_KV_HEADS, QK_DIM), vs.reshape(1, dec_b, 1, N_KV_HEADS, V_DIM),
            hs_re.reshape(1, dec_b, *st), hs_im.reshape(1, dec_b, *st))
```

```python
import functools
import math

import jax
import jax.numpy as jnp
from jax import lax
from jax.experimental import pallas as pl
from jax.experimental.pallas import tpu as pltpu

F32 = jnp.float32
BF16 = jnp.bfloat16

D_MODEL = 1024
N_HEADS = 8
N_KV_HEADS = 4
GQA_GROUP = N_HEADS // N_KV_HEADS
HD = 64
QK_DIM = 2 * HD
V_DIM = 2 * HD
ROT_DIM = HD // 4
ROPE_THETA = 500000.0
SSM_GROUP = 16
SSM_GROUPS = D_MODEL // SSM_GROUP
SSM_STATE = 64
SSM_STATES = SSM_GROUPS * SSM_STATE
D_FF = 4 * D_MODEL
DEPTH = 1
ALPHA = (2.0 * DEPTH) ** 0.25
LN_EPS = 1e-5
LAM_INIT = 0.8 - 0.6 * math.exp(-0.3 * 0)
Q_COLS = N_HEADS * QK_DIM
K_COLS = N_KV_HEADS * QK_DIM
V_COLS = N_KV_HEADS * V_DIM
IN_COLS = Q_COLS + K_COLS + V_COLS + 3 * D_MODEL

LANES = 128
SSM_CHUNK = 16
CHUNK_COLS = SSM_CHUNK * SSM_GROUP
PAIR_STATES = 2 * SSM_STATE
VMEM_LIMIT = 48 * 1024 * 1024

_NT = (((1,), (1,)), ((), ()))


def _cparams(n_axes):
    return pltpu.CompilerParams(dimension_semantics=("arbitrary",) * n_axes, vmem_limit_bytes=VMEM_LIMIT)


def _const_spec(shape):
    nd = len(shape)
    return pl.BlockSpec(shape, lambda *_: (0,) * nd, pipeline_mode=pl.Buffered(1))


def _layer_norm(x):
    mu = jnp.mean(x, axis=-1, keepdims=True)
    xc = x - mu
    var = jnp.mean(xc * xc, axis=-1, keepdims=True)
    return xc * lax.rsqrt(var + LN_EPS)


def _ada_kernel(c_ref, w_ref, b_ref, o_ref):
    c = c_ref[...]
    s = c * jax.nn.sigmoid(c)
    o_ref[...] = jnp.dot(s.astype(BF16), w_ref[...], preferred_element_type=F32) + b_ref[...]


def _ada(c_all, w_ada, b_ada):
    rows = c_all.shape[0]
    tn = D_MODEL
    return pl.pallas_call(
        _ada_kernel,
        grid=(6 * D_MODEL // tn,),
        in_specs=[pl.BlockSpec((rows, D_MODEL), lambda j: (0, 0)),
                  pl.BlockSpec((D_MODEL, tn), lambda j: (0, j)),
                  pl.BlockSpec((1, tn), lambda j: (0, j))],
        out_specs=pl.BlockSpec((rows, tn), lambda j: (0, j)),
        out_shape=jax.ShapeDtypeStruct((rows, 6 * D_MODEL), F32),
        compiler_params=_cparams(1),
    )(c_all, w_ada, b_ada)


GROUPS_PER_TILE = LANES // SSM_GROUP
CHUNKS_PER_TILE = LANES // SSM_GROUP


def _lane_group():
    return lax.broadcasted_iota(jnp.int32, (1, LANES), 1) // SSM_GROUP


def _to_chunk_layout(u_s, uc_ref):
    n_chunks = u_s.shape[1] // SSM_CHUNK
    lane_grp = _lane_group()
    for jb in range(SSM_GROUPS // GROUPS_PER_TILE):
        for half in range(SSM_CHUNK // CHUNKS_PER_TILE):
            acc = [None] * GROUPS_PER_TILE
            for t8 in range(CHUNKS_PER_TILE):
                src = u_s[jb, pl.ds(half * CHUNKS_PER_TILE + t8, n_chunks, stride=SSM_CHUNK), :]
                for gl in range(GROUPS_PER_TILE):
                    shift = ((t8 - gl) * SSM_GROUP) % LANES
                    r = pltpu.roll(src, shift, 1) if shift else src
                    acc[gl] = r if t8 == 0 else jnp.where(lane_grp == t8, r, acc[gl])
            for gl in range(GROUPS_PER_TILE):
                uc_ref[jb * GROUPS_PER_TILE + gl, :, half * LANES:(half + 1) * LANES] = acc[gl].astype(BF16)


def _from_chunk_layout(yc_ref, y_s):
    n_chunks = yc_ref.shape[1]
    lane_grp = _lane_group()
    for jb in range(SSM_GROUPS // GROUPS_PER_TILE):
        acc = [None] * SSM_CHUNK
        for gl in range(GROUPS_PER_TILE):
            g = jb * GROUPS_PER_TILE + gl
            for half in range(SSM_CHUNK // CHUNKS_PER_TILE):
                c0 = (g % 2) * CHUNK_COLS + half * LANES
                src = yc_ref[g // 2, :, c0:c0 + LANES].astype(F32)
                for t8 in range(CHUNKS_PER_TILE):
                    tl = half * CHUNKS_PER_TILE + t8
                    shift = ((gl - t8) * SSM_GROUP) % LANES
                    r = pltpu.roll(src, shift, 1) if shift else src
                    acc[tl] = r if gl == 0 else jnp.where(lane_grp == gl, r, acc[tl])
        for tl in range(SSM_CHUNK):
            y_s[jb, pl.ds(tl, n_chunks, stride=SSM_CHUNK), :] = acc[tl]


def _proj_kernel(x_ref, sc_ref, sh_ref, cos_ref, sa_ref, sb_ref, w_ref,
                 q_ref, k_ref, v_ref, kb_ref, vb_ref, u_ref, ga_ref, gs_ref, *maybe_chunked):
    h = (_layer_norm(x_ref[...]) * (1.0 + sc_ref[0]) + sh_ref[0]).astype(BF16)
    cos_t, sin_a, sin_b = cos_ref[...], sa_ref[...], sb_ref[...]

    def rope(t):
        up = pltpu.roll(t, LANES - ROT_DIM // 2, 1)
        dn = pltpu.roll(t, ROT_DIM // 2, 1)
        return t * cos_t + up * sin_a + dn * sin_b

    def mm(c0, c1):
        return jnp.dot(h, w_ref[:, c0:c1], preferred_element_type=F32)

    q = mm(0, Q_COLS)
    for s in range(Q_COLS // LANES):
        sl = slice(s * LANES, (s + 1) * LANES)
        q_ref[:, sl] = (rope(q[:, sl]) * (1.0 / math.sqrt(HD))).astype(BF16)
    k = mm(Q_COLS, Q_COLS + K_COLS)
    for s in range(K_COLS // LANES):
        sl = slice(s * LANES, (s + 1) * LANES)
        kr = rope(k[:, sl])
        k_ref[:, sl] = kr
        kb_ref[:, sl] = kr.astype(BF16)
    c0 = Q_COLS + K_COLS
    v = mm(c0, c0 + V_COLS)
    v_ref[...] = v
    vb_ref[...] = v.astype(BF16)
    c0 += V_COLS
    u = mm(c0, c0 + D_MODEL)
    u_ref[...] = u
    c0 += D_MODEL
    ga_ref[...] = mm(c0, c0 + D_MODEL).astype(BF16)
    c0 += D_MODEL
    gs_ref[...] = mm(c0, c0 + D_MODEL).astype(BF16)
    if maybe_chunked:
        uc_ref, u_s = maybe_chunked
        for jb in range(D_MODEL // LANES):
            u_s[jb] = u[:, jb * LANES:(jb + 1) * LANES]
        _to_chunk_layout(u_s, uc_ref)


def _proj(x, sc, sh, tables, w_in, *, tm, rows_per_mod, table_blocks, chunk_u):
    T = x.shape[0]
    mod_rows = sc.shape[1]
    mod_spec = pl.BlockSpec((1, mod_rows, D_MODEL), lambda i: (i * tm // rows_per_mod, 0, 0))
    tab_spec = pl.BlockSpec((tm, LANES), lambda i: (i % table_blocks, 0))
    row = lambda w: pl.BlockSpec((tm, w), lambda i: (i, 0))
    outs = [(Q_COLS, BF16), (K_COLS, F32), (V_COLS, F32), (K_COLS, BF16), (V_COLS, BF16),
            (D_MODEL, F32), (D_MODEL, BF16), (D_MODEL, BF16)]
    out_specs = [row(w) for w, _ in outs]
    out_shape = [jax.ShapeDtypeStruct((T, w), dt) for w, dt in outs]
    if chunk_u:
        out_specs.append(pl.BlockSpec((SSM_GROUPS, tm // SSM_CHUNK, CHUNK_COLS), lambda i: (0, i, 0)))
        out_shape.append(jax.ShapeDtypeStruct((SSM_GROUPS, T // SSM_CHUNK, CHUNK_COLS), BF16))
    return pl.pallas_call(
        _proj_kernel,
        grid=(T // tm,),
        in_specs=[row(D_MODEL), mod_spec, mod_spec, tab_spec, tab_spec, tab_spec,
                  _const_spec((D_MODEL, IN_COLS))],
        out_specs=out_specs,
        out_shape=out_shape,
        scratch_shapes=[pltpu.VMEM((D_MODEL // LANES, tm, LANES), F32)] if chunk_u else [],
        compiler_params=_cparams(1),
    )(x, sc, sh, *tables, w_in)


def _rope_tables(pos):
    inv = ROPE_THETA ** (-jnp.arange(0, ROT_DIM, 2, dtype=F32) / ROT_DIM)
    ang = pos.astype(F32)[:, None] * inv[None, :]
    cos, sin = jnp.cos(ang), jnp.sin(ang)
    n, half = pos.shape[0], ROT_DIM // 2
    ones = jnp.ones((n, HD - ROT_DIM), F32)
    zeros = lambda w: jnp.zeros((n, w), F32)
    cos_t = jnp.concatenate([cos, cos, ones], axis=-1)
    sin_a = jnp.concatenate([-sin, zeros(HD - half)], axis=-1)
    sin_b = jnp.concatenate([zeros(half), sin, zeros(HD - ROT_DIM)], axis=-1)
    return tuple(jnp.tile(t, (1, LANES // HD)) for t in (cos_t, sin_a, sin_b))


def _lambda(lq1, lk1, lq2, lk2):
    return (jnp.exp(jnp.sum(lq1[...] * lk1[...], axis=-1, keepdims=True))
            - jnp.exp(jnp.sum(lq2[...] * lk2[...], axis=-1, keepdims=True)) + LAM_INIT)


def _sub_norm(o, subw):
    o = o * lax.rsqrt(jnp.mean(o * o, axis=-1, keepdims=True) + LN_EPS)
    return o * subw * (1.0 - LAM_INIT)


N_MAPS = 2 * GQA_GROUP


def _attn_kernel(lq1, lk1, lq2, lk2, subw_ref, q_ref, k_ref, v_ref, o_ref,
                 q4_s, s_s, mpart_s, m_s, lpart_s, acc_s, *, tq, tk):
    qi = pl.program_id(2)
    rows = N_MAPS * tq
    comp0 = lax.broadcasted_iota(jnp.int32, (1, QK_DIM), 1) < HD
    for g in range(GQA_GROUP):
        qg = q_ref[:, g * QK_DIM:(g + 1) * QK_DIM].astype(F32)
        q4_s[(2 * g) * tq:(2 * g + 1) * tq, :] = jnp.where(comp0, qg, 0.0).astype(BF16)
        q4_s[(2 * g + 1) * tq:(2 * g + 2) * tq, :] = jnp.where(comp0, 0.0, qg).astype(BF16)
    n_blk = ((qi + 1) * tq - 1) // tk + 1
    mpart_s[...] = jnp.full(mpart_s.shape, -jnp.inf, F32)

    def scores(ki, masked):
        start = pl.multiple_of(ki * tk, tk)
        s = lax.dot_general(q4_s[...], k_ref[pl.ds(start, tk), :], _NT, preferred_element_type=F32)
        if masked:
            q_pos = qi * tq + jnp.bitwise_and(lax.broadcasted_iota(jnp.int32, (rows, tk), 0), tq - 1)
            k_pos = start + lax.broadcasted_iota(jnp.int32, (rows, tk), 1)
            s = jnp.where(q_pos >= k_pos, s, -jnp.inf)
        s_s[ki] = s
        m = mpart_s[...]
        for j in range(tk // LANES):
            m = jnp.maximum(m, s[:, j * LANES:(j + 1) * LANES])
        mpart_s[...] = m

    def scores_body(ki, carry):
        scores(ki, False)
        return carry

    lax.fori_loop(0, n_blk - 1, scores_body, 0)
    scores(n_blk - 1, True)

    m_s[...] = jnp.broadcast_to(jnp.max(mpart_s[...], axis=-1, keepdims=True), m_s.shape)
    lpart_s[...] = jnp.zeros(lpart_s.shape, F32)
    acc_s[...] = jnp.zeros(acc_s.shape, F32)

    def weights_body(ki, carry):
        start = pl.multiple_of(ki * tk, tk)
        s = s_s[ki]
        m = m_s[...]
        lp = lpart_s[...]
        ps = []
        for j in range(tk // LANES):
            pj = jnp.exp(s[:, j * LANES:(j + 1) * LANES] - m)
            lp += pj
            ps.append(pj.astype(BF16))
        lpart_s[...] = lp
        acc_s[...] += jnp.dot(jnp.concatenate(ps, axis=-1), v_ref[pl.ds(start, tk), :], preferred_element_type=F32)
        return carry

    lax.fori_loop(0, n_blk, weights_body, 0)

    lam = _lambda(lq1, lk1, lq2, lk2)
    o = acc_s[...] / jnp.sum(lpart_s[...], axis=-1, keepdims=True)
    for g in range(GQA_GROUP):
        d = o[(2 * g) * tq:(2 * g + 1) * tq] - lam * o[(2 * g + 1) * tq:(2 * g + 2) * tq]
        o_ref[:, g * V_DIM:(g + 1) * V_DIM] = _sub_norm(d, subw_ref[...]).astype(BF16)


def _prompt_attention(q, kb, vb, lams, subw, *, batch, seq, tq, tk):
    assert tk % tq == 0 and tq & (tq - 1) == 0
    T = batch * seq
    nq = seq // tq
    rows = N_MAPS * tq
    small = pl.BlockSpec((1, HD), lambda b, n, i: (0, 0))
    return pl.pallas_call(
        functools.partial(_attn_kernel, tq=tq, tk=tk),
        grid=(batch, N_KV_HEADS, nq),
        in_specs=[small, small, small, small,
                  pl.BlockSpec((1, V_DIM), lambda b, n, i: (0, 0)),
                  pl.BlockSpec((tq, GQA_GROUP * QK_DIM), lambda b, n, i: (b * nq + i, n)),
                  pl.BlockSpec((seq, QK_DIM), lambda b, n, i: (b, n)),
                  pl.BlockSpec((seq, V_DIM), lambda b, n, i: (b, n))],
        out_specs=pl.BlockSpec((tq, GQA_GROUP * V_DIM), lambda b, n, i: (b * nq + i, n)),
        out_shape=jax.ShapeDtypeStruct((T, N_HEADS * V_DIM), BF16),
        scratch_shapes=[pltpu.VMEM((rows, QK_DIM), BF16),
                        pltpu.VMEM((seq // tk, rows, tk), F32),
                        pltpu.VMEM((rows, LANES), F32),
                        pltpu.VMEM((rows, LANES), F32),
                        pltpu.VMEM((rows, LANES), F32),
                        pltpu.VMEM((rows, V_DIM), F32)],
        compiler_params=_cparams(3),
    )(*lams, subw, q, kb, vb)


DEC_PAGES_PER_STEP = 8
DEC_ROWS = N_KV_HEADS * N_MAPS


def _decode_kernel(pt_ref, lq1, lk1, lq2, lk2, subw_ref, qa_ref, knew_ref, vnew_ref, *rest):
    del pt_ref
    npg = DEC_PAGES_PER_STEP
    k_refs, v_refs = rest[:npg], rest[npg:2 * npg]
    o_ref, m_s, l_s, acc_s = rest[2 * npg:]
    c = pl.program_id(1)
    qa = qa_ref[0]
    page_rows = k_refs[0].shape[0]

    @pl.when(c == 0)
    def _():
        m_s[...] = jnp.sum(qa.astype(F32) * knew_ref[0], axis=-1, keepdims=True)
        l_s[...] = jnp.ones(l_s.shape, F32)
        acc_s[...] = vnew_ref[0]

    own_head = (jnp.bitwise_and(lax.broadcasted_iota(jnp.int32, (DEC_ROWS, page_rows), 1), N_KV_HEADS - 1)
                == lax.broadcasted_iota(jnp.int32, (DEC_ROWS, page_rows), 0) // N_MAPS)
    s = jnp.concatenate(
        [jnp.where(own_head, lax.dot_general(qa, k_refs[j][...].astype(BF16), _NT, preferred_element_type=F32),
                   -jnp.inf) for j in range(npg)], axis=-1)
    m_prev = m_s[...]
    m_new = jnp.maximum(m_prev, jnp.max(s, axis=-1, keepdims=True))
    alpha = jnp.exp(m_prev - m_new)
    p = jnp.exp(s - m_new)
    l_s[...] = alpha * l_s[...] + jnp.sum(p, axis=-1, keepdims=True)
    pv = jnp.dot(p[:, :page_rows].astype(BF16), v_refs[0][...].astype(BF16), preferred_element_type=F32)
    for j in range(1, npg):
        pv += jnp.dot(p[:, j * page_rows:(j + 1) * page_rows].astype(BF16), v_refs[j][...].astype(BF16),
                      preferred_element_type=F32)
    acc_s[...] = alpha * acc_s[...] + pv
    m_s[...] = m_new

    @pl.when(c == pl.num_programs(1) - 1)
    def _():
        lam = _lambda(lq1, lk1, lq2, lk2)
        o = acc_s[...] / l_s[...]
        for n in range(N_KV_HEADS):
            r = n * N_MAPS
            d = o[r:r + GQA_GROUP] - lam * o[r + GQA_GROUP:r + 2 * GQA_GROUP]
            o_ref[0, n * GQA_GROUP:(n + 1) * GQA_GROUP, :] = _sub_norm(d, subw_ref[...])


def _decode_attention(q, k_new, v_new, cache_k, cache_v, page_table, lams, subw, layer):
    bsz = q.shape[0]
    page = cache_k.shape[2]
    n_pages = page_table.shape[1]
    npg = DEC_PAGES_PER_STEP
    q5 = q.reshape(bsz, N_KV_HEADS, GQA_GROUP, 2, HD)
    qa = jnp.einsum('bngcd,ce->bncged', q5, jnp.eye(2, dtype=q.dtype)).reshape(bsz, DEC_ROWS, QK_DIM)
    per_row = lambda t: jnp.repeat(t.reshape(bsz, N_KV_HEADS, QK_DIM), N_MAPS, axis=1)
    small = pl.BlockSpec((1, HD), lambda b, c, pt: (0, 0))
    rows_spec = pl.BlockSpec((1, DEC_ROWS, QK_DIM), lambda b, c, pt: (b, 0, 0))
    n_pool = cache_k.shape[1]
    cache_k = cache_k.reshape(-1, page * N_KV_HEADS, QK_DIM)
    cache_v = cache_v.reshape(-1, page * N_KV_HEADS, V_DIM)
    page_specs = [pl.BlockSpec((None, page * N_KV_HEADS, QK_DIM),
                               functools.partial(lambda b, c, pt, j: (layer * n_pool + pt[b, c * npg + j], 0, 0), j=j))
                  for j in range(npg)]
    grid_spec = pltpu.PrefetchScalarGridSpec(
        num_scalar_prefetch=1,
        grid=(bsz, n_pages // npg),
        in_specs=[small, small, small, small,
                  pl.BlockSpec((1, V_DIM), lambda b, c, pt: (0, 0)),
                  rows_spec, rows_spec, rows_spec] + page_specs + page_specs,
        out_specs=pl.BlockSpec((1, N_HEADS, V_DIM), lambda b, c, pt: (b, 0, 0)),
        scratch_shapes=[pltpu.VMEM((DEC_ROWS, 1), F32), pltpu.VMEM((DEC_ROWS, 1), F32),
                        pltpu.VMEM((DEC_ROWS, V_DIM), F32)])
    o = pl.pallas_call(
        _decode_kernel,
        grid_spec=grid_spec,
        out_shape=jax.ShapeDtypeStruct((bsz, N_HEADS, V_DIM), F32),
        compiler_params=_cparams(2),
    )(page_table, *lams, subw, qa, per_row(k_new), per_row(v_new), *([cache_k] * npg), *([cache_v] * npg))
    return o.reshape(bsz, N_HEADS * V_DIM)


N_POW = 24


def _ssm_prep_kernel(lr_ref, li_ref, ldt_ref, cre_ref, cim_ref, bre_ref, bim_ref,
                     kern_ref, wsr_ref, wsi_ref, gre_ref, gim_ref, a1r_ref, a1i_ref, acr_ref, aci_ref,
                     bbr_ref, bbi_ref):
    lr, li = lr_ref[0], li_ref[0]
    dt = jnp.exp(ldt_ref[0])
    ell = lax.broadcasted_iota(jnp.int32, (N_POW, 1), 0).astype(F32)
    mag = jnp.exp(ell * (lr * dt))
    ph = ell * (li * dt)
    p_re, p_im = mag * jnp.cos(ph), mag * jnp.sin(ph)
    a_re, a_im = p_re[1:2], p_im[1:2]
    a1r_ref[0], a1i_ref[0] = a_re, a_im
    acr_ref[0], aci_ref[0] = p_re[SSM_CHUNK:SSM_CHUNK + 1], p_im[SSM_CHUNK:SSM_CHUNK + 1]
    den = lr * lr + li * li
    nr, ni = a_re - 1.0, a_im
    co_re = (nr * lr + ni * li) / den
    co_im = (ni * lr - nr * li) / den
    b_re, b_im = bre_ref[0], bim_ref[0]
    bb_re = co_re * b_re - co_im * b_im
    bb_im = co_re * b_im + co_im * b_re
    bbr_ref[0], bbi_ref[0] = bb_re, bb_im
    c_re, c_im = cre_ref[0], cim_ref[0]
    for l in range(SSM_CHUNK + 1):
        rows = slice(l * SSM_GROUP, (l + 1) * SSM_GROUP)
        gre_ref[0, rows, :] = c_re * p_re[l:l + 1] - c_im * p_im[l:l + 1]
        gim_ref[0, rows, :] = c_re * p_im[l:l + 1] + c_im * p_re[l:l + 1]
    for s in range(SSM_CHUNK):
        rows = slice(s * SSM_GROUP, (s + 1) * SSM_GROUP)
        e = SSM_CHUNK - 1 - s
        wsr_ref[0, rows, :] = p_re[e:e + 1] * bb_re - p_im[e:e + 1] * bb_im
        wsi_ref[0, rows, :] = p_re[e:e + 1] * bb_im + p_im[e:e + 1] * bb_re
    g_re = gre_ref[0, :CHUNK_COLS, :]
    g_im = gim_ref[0, :CHUNK_COLS, :]
    hi = lax.Precision.HIGHEST
    kern_ref[0] = (lax.dot_general(g_re, bb_re, _NT, precision=hi, preferred_element_type=F32)
                   - lax.dot_general(g_im, bb_im, _NT, precision=hi, preferred_element_type=F32))


def _ssm_prep(lam_re, lam_im, log_dt, b_re, b_im, c_re, c_im):
    G, P, C = SSM_GROUPS, SSM_STATE, SSM_GROUP
    g_rows = (SSM_CHUNK + 1) * C
    vec = pl.BlockSpec((1, 1, P), lambda g: (g, 0, 0))
    mat = pl.BlockSpec((1, C, P), lambda g: (g, 0, 0))
    out_defs = [((CHUNK_COLS, C), None), ((CHUNK_COLS, P), None), ((CHUNK_COLS, P), None),
                ((g_rows, P), None), ((g_rows, P), None),
                ((1, P), None), ((1, P), None), ((1, P), None), ((1, P), None), ((C, P), None), ((C, P), None)]
    return pl.pallas_call(
        _ssm_prep_kernel,
        grid=(G,),
        in_specs=[vec, vec, pl.BlockSpec((1, 1, 1), lambda g: (g, 0, 0)), mat, mat, mat, mat],
        out_specs=[pl.BlockSpec((1,) + s, lambda g: (g, 0, 0)) for s, _ in out_defs],
        out_shape=[jax.ShapeDtypeStruct((G,) + s, F32) for s, _ in out_defs],
        compiler_params=_cparams(1),
    )(lam_re.reshape(G, 1, P), lam_im.reshape(G, 1, P), log_dt.reshape(G, 1, 1),
      c_re, c_im, jnp.swapaxes(b_re, 1, 2), jnp.swapaxes(b_im, 1, 2))


def _pair_block_diag(w):
    g, a, b = w.shape
    w2 = w.reshape(g // 2, 2, a, b)
    return jnp.einsum('qjab,jk->qjakb', w2, jnp.eye(2, dtype=w.dtype)).reshape(g // 2, 2 * a, 2 * b)


def _tile_block_diag(w, per_tile):
    g, a, b = w.shape
    w2 = w.reshape(g // per_tile, per_tile, a, b)
    eye = jnp.eye(per_tile, dtype=w.dtype)
    return jnp.einsum('qjab,jk->qjakb', w2, eye).reshape(g // per_tile, per_tile * a, per_tile * b)


def _ssm_state_kernel(uc_ref, wsr_ref, wsi_ref, sre_ref, sim_ref):
    def both(w_ref):
        return jnp.concatenate([jnp.dot(uc_ref[j], w_ref[j], preferred_element_type=F32) for j in range(2)], axis=-1)
    sre_ref[...] = both(wsr_ref)
    sim_ref[...] = both(wsi_ref)


def _ssm_scan_kernel(ar_ref, ai_ref, sre_ref, sim_ref, hpr_ref, hpi_ref, hfr_ref, hfi_ref):
    ar, ai = ar_ref[...], ai_ref[...]
    batch, n_chunks, width = sre_ref.shape

    def body(k, carry):
        nxt = []
        for b in range(batch):
            hr, hi = carry[2 * b], carry[2 * b + 1]
            row = (b, pl.ds(k, 1), slice(None))
            hpr_ref[row] = hr
            hpi_ref[row] = hi
            nxt += [ar * hr - ai * hi + sre_ref[row], ar * hi + ai * hr + sim_ref[row]]
        return tuple(nxt)

    zero = jnp.zeros((1, width), F32)
    fin = lax.fori_loop(0, n_chunks, body, (zero,) * (2 * batch))
    for b in range(batch):
        hfr_ref[b:b + 1, :] = fin[2 * b]
        hfi_ref[b:b + 1, :] = fin[2 * b + 1]


def _ssm_out_kernel(uc_ref, toep_ref, hpr_ref, hpi_ref, wcr_ref, wci_ref, y_ref):
    y = jnp.concatenate([jnp.dot(uc_ref[j], toep_ref[j], preferred_element_type=F32) for j in range(2)], axis=-1)
    y += jnp.dot(hpr_ref[...].astype(BF16), wcr_ref[0], preferred_element_type=F32)
    y += jnp.dot(hpi_ref[...].astype(BF16), wci_ref[0], preferred_element_type=F32)
    y_ref[0] = y.astype(BF16)


def _prompt_ssm(uc, prep, *, batch, seq):
    kern, wsr, wsi, g_re, g_im, _, _, ac_re, ac_im, _, _ = prep
    G, P, C, TL = SSM_GROUPS, SSM_STATE, SSM_GROUP, SSM_CHUNK
    nk = seq // TL
    rows = nk * batch
    k4 = kern.reshape(G, TL, C, C)
    lag = jnp.arange(TL)[None, :] - jnp.arange(TL)[:, None]
    toep = jnp.where((lag >= 0)[None, :, :, None, None], k4[:, jnp.clip(lag, 0, TL - 1)], 0.0)
    toep = toep.transpose(0, 1, 4, 2, 3).reshape(G, CHUNK_COLS, CHUNK_COLS).astype(BF16)
    wc_re = g_re.reshape(G, TL + 1, C, P)[:, 1:].transpose(0, 3, 1, 2).reshape(G, P, CHUNK_COLS)
    wc_im = -g_im.reshape(G, TL + 1, C, P)[:, 1:].transpose(0, 3, 1, 2).reshape(G, P, CHUNK_COLS)
    wc_re, wc_im = _pair_block_diag(wc_re).astype(BF16), _pair_block_diag(wc_im).astype(BF16)

    pair3 = lambda a, b: pl.BlockSpec((2, a, b), lambda q: (q, 0, 0))
    s_spec = pl.BlockSpec((rows, PAIR_STATES), lambda q: (0, q))
    s_re, s_im = pl.pallas_call(
        _ssm_state_kernel,
        grid=(G // 2,),
        in_specs=[pair3(rows, CHUNK_COLS), pair3(CHUNK_COLS, P), pair3(CHUNK_COLS, P)],
        out_specs=[s_spec, s_spec],
        out_shape=[jax.ShapeDtypeStruct((rows, G * P), F32)] * 2,
        compiler_params=_cparams(1),
    )(uc, wsr.astype(BF16), wsi.astype(BF16))

    wb = 512
    a_spec = pl.BlockSpec((1, wb), lambda j: (0, j))
    seq_spec = pl.BlockSpec((batch, nk, wb), lambda j: (0, 0, j))
    fin_spec = pl.BlockSpec((batch, wb), lambda j: (0, j))
    hp_re, hp_im, hf_re, hf_im = pl.pallas_call(
        _ssm_scan_kernel,
        grid=(G * P // wb,),
        in_specs=[a_spec, a_spec, seq_spec, seq_spec],
        out_specs=[seq_spec, seq_spec, fin_spec, fin_spec],
        out_shape=[jax.ShapeDtypeStruct((batch, nk, G * P), F32)] * 2 + [jax.ShapeDtypeStruct((batch, G * P), F32)] * 2,
        compiler_params=_cparams(1),
    )(ac_re.reshape(1, G * P), ac_im.reshape(1, G * P),
      s_re.reshape(batch, nk, G * P), s_im.reshape(batch, nk, G * P))

    yc = pl.pallas_call(
        _ssm_out_kernel,
        grid=(G // 2,),
        in_specs=[pair3(rows, CHUNK_COLS), pair3(CHUNK_COLS, CHUNK_COLS), s_spec, s_spec,
                  pl.BlockSpec((1, PAIR_STATES, 2 * CHUNK_COLS), lambda q: (q, 0, 0)),
                  pl.BlockSpec((1, PAIR_STATES, 2 * CHUNK_COLS), lambda q: (q, 0, 0))],
        out_specs=pl.BlockSpec((1, rows, 2 * CHUNK_COLS), lambda q: (q, 0, 0)),
        out_shape=jax.ShapeDtypeStruct((G // 2, rows, 2 * CHUNK_COLS), BF16),
        compiler_params=_cparams(1),
    )(uc, toep, hp_re.reshape(rows, G * P), hp_im.reshape(rows, G * P), wc_re, wc_im)
    return yc, hf_re, hf_im


SSM_TILE_GROUPS = 16


def _ssm_step_kernel(u_ref, h0r_ref, h0i_ref, ar_ref, ai_ref, bdr_ref, bdi_ref, cdr_ref, cdi_ref,
                     hr_ref, hi_ref, y_ref):
    ch = SSM_TILE_GROUPS * SSM_GROUP
    st = SSM_TILE_GROUPS * SSM_STATE
    for t in range(SSM_GROUPS // SSM_TILE_GROUPS):
        cs, ss = slice(t * ch, (t + 1) * ch), slice(t * st, (t + 1) * st)
        ub = u_ref[:, cs].astype(BF16)
        ar, ai, h0r, h0i = ar_ref[:, ss], ai_ref[:, ss], h0r_ref[:, ss], h0i_ref[:, ss]
        hr = ar * h0r - ai * h0i + jnp.dot(ub, bdr_ref[t], preferred_element_type=F32)
        hi = ar * h0i + ai * h0r + jnp.dot(ub, bdi_ref[t], preferred_element_type=F32)
        hr_ref[:, ss] = hr
        hi_ref[:, ss] = hi
        y_ref[:, cs] = (jnp.dot(hr.astype(BF16), cdr_ref[t], preferred_element_type=F32)
                        + jnp.dot(hi.astype(BF16), cdi_ref[t], preferred_element_type=F32))


def _sample_ssm(u, h0_re, h0_im, prep, c_re, c_im):
    _, _, _, _, _, a_re, a_im, _, _, bb_re, bb_im = prep
    bsz = u.shape[0]
    n_st = SSM_STATES
    bd_re = _tile_block_diag(bb_re, SSM_TILE_GROUPS).astype(BF16)
    bd_im = _tile_block_diag(bb_im, SSM_TILE_GROUPS).astype(BF16)
    cd_re = _tile_block_diag(jnp.swapaxes(c_re, 1, 2), SSM_TILE_GROUPS).astype(BF16)
    cd_im = _tile_block_diag(-jnp.swapaxes(c_im, 1, 2), SSM_TILE_GROUPS).astype(BF16)
    args = (u, h0_re.reshape(bsz, n_st), h0_im.reshape(bsz, n_st), a_re.reshape(1, n_st), a_im.reshape(1, n_st),
            bd_re, bd_im, cd_re, cd_im)
    return pl.pallas_call(
        _ssm_step_kernel,
        grid=(1,),
        in_specs=[_const_spec(a.shape) for a in args],
        out_specs=[pl.BlockSpec((bsz, n_st), lambda i: (0, 0))] * 2 + [pl.BlockSpec((bsz, D_MODEL), lambda i: (0, 0))],
        out_shape=[jax.ShapeDtypeStruct((bsz, n_st), F32)] * 2 + [jax.ShapeDtypeStruct((bsz, D_MODEL), F32)],
        compiler_params=_cparams(1),
    )(*args)


FF_CHUNK = 1024


def _merge_kernel(x_ref, o_ref, ga_ref, gs_ref, y_ref, u_ref, g1_ref, sc2_ref, sh2_ref, g2_ref,
                  d_ref, wglu_ref, bglu_ref, wo_ref, ln1g_ref, ln1b_ref,
                  wup_ref, bup_ref, wdn_ref, bdn_ref, ln2g_ref, ln2b_ref, out_ref, *maybe_y_s):
    if maybe_y_s:
        y_s = maybe_y_s[0]
        _from_chunk_layout(y_ref, y_s)
        y = jnp.concatenate([y_s[jb] for jb in range(D_MODEL // LANES)], axis=-1)
    else:
        y = y_ref[...]
    yd = y + d_ref[...] * u_ref[...]
    g = jax.nn.gelu(yd, approximate=True)
    z = jnp.dot(g.astype(BF16), wglu_ref[...], preferred_element_type=F32) + bglu_ref[...]
    y_ssm = g * jax.nn.sigmoid(z)
    mixed = (jax.nn.sigmoid(ga_ref[...].astype(F32)) * o_ref[...].astype(F32)
             + jax.nn.sigmoid(gs_ref[...].astype(F32)) * y_ssm)
    t = jnp.dot(mixed.astype(BF16), wo_ref[...], preferred_element_type=F32)
    x1 = _layer_norm(ALPHA * x_ref[...] + (1.0 + g1_ref[0]) * t) * ln1g_ref[...] + ln1b_ref[...]
    h2 = (_layer_norm(x1) * (1.0 + sc2_ref[0]) + sh2_ref[0]).astype(BF16)
    f = jnp.zeros(x1.shape, F32)
    for c in range(D_FF // FF_CHUNK):
        cs = slice(c * FF_CHUNK, (c + 1) * FF_CHUNK)
        a = jnp.maximum(jnp.dot(h2, wup_ref[:, cs], preferred_element_type=F32) + bup_ref[:, cs], 0.0)
        f += jnp.dot((a * a).astype(BF16), wdn_ref[cs, :], preferred_element_type=F32)
    f += bdn_ref[...]
    out_ref[...] = _layer_norm(ALPHA * x1 + (1.0 + g2_ref[0]) * f) * ln2g_ref[...] + ln2b_ref[...]


def _merge(x, o, ga, gs, y, u, mods, params, *, tm, rows_per_mod):
    T = x.shape[0]
    mod_rows = mods[0].shape[1]
    mod_spec = pl.BlockSpec((1, mod_rows, D_MODEL), lambda i: (i * tm // rows_per_mod, 0, 0))
    row = pl.BlockSpec((tm, D_MODEL), lambda i: (i, 0))
    chunked = y.ndim == 3
    y_spec = pl.BlockSpec((SSM_GROUPS // 2, tm // SSM_CHUNK, 2 * CHUNK_COLS), lambda i: (0, i, 0)) if chunked else row
    return pl.pallas_call(
        _merge_kernel,
        grid=(T // tm,),
        in_specs=[row] * 4 + [y_spec, row] + [mod_spec] * 4 + [_const_spec(p.shape) for p in params],
        out_specs=row,
        out_shape=jax.ShapeDtypeStruct((T, D_MODEL), F32),
        scratch_shapes=[pltpu.VMEM((D_MODEL // LANES, tm, LANES), F32)] if chunked else [],
        compiler_params=_cparams(1),
    )(x, o, ga, gs, y, u, *mods, *params)


def _split_mods(ada):
    return [ada[..., i * D_MODEL:(i + 1) * D_MODEL] for i in range(6)]


def kernel(x_prompt, x_sample, cache_k, cache_v, state_ssm_re, state_ssm_im, page_table, c_prompt, c_sample, w_in, lambda_q1, lambda_k1, lambda_q2, lambda_k2, subln_w, ssm_lambda_re, ssm_lambda_im, ssm_log_dt, ssm_b_re, ssm_b_im, ssm_c_re, ssm_c_im, ssm_d, w_glu, b_glu, w_o, w_ada, b_ada, ln1_g, ln1_b, ln2_g, ln2_b, w_up, b_up, w_down, b_down):
    assert w_in.shape[0] == DEPTH and x_sample.shape[1] == 1
    batch, seq, _ = x_prompt.shape
    dec_b = x_sample.shape[0]
    page = cache_k.shape[2]
    past_len = page_table.shape[1] * page
    l = 0
    row2 = lambda a: a[l].reshape(1, -1)

    pad = (-batch) % 8
    c_all = jnp.concatenate([c_prompt, jnp.zeros((pad, D_MODEL), F32), c_sample], axis=0)
    ada = _ada(c_all, w_ada[l].astype(BF16), row2(b_ada))
    mods_p = _split_mods(ada[:batch].reshape(batch, 1, 6 * D_MODEL))
    mods_s = _split_mods(ada[batch + pad:].reshape(1, dec_b, 6 * D_MODEL))

    w_in_b = w_in[l].astype(BF16)
    lams = (row2(lambda_q1), row2(lambda_k1), row2(lambda_q2), row2(lambda_k2))
    subw = row2(subln_w)
    merge_params = (row2(ssm_d), w_glu[l].astype(BF16), row2(b_glu), w_o[l].astype(BF16), row2(ln1_g), row2(ln1_b),
                    w_up[l].astype(BF16), row2(b_up), w_down[l].astype(BF16), row2(b_down), row2(ln2_g), row2(ln2_b))
    prep = _ssm_prep(ssm_lambda_re[l], ssm_lambda_im[l], ssm_log_dt[l], ssm_b_re[l], ssm_b_im[l],
                     ssm_c_re[l], ssm_c_im[l])

    tm = 512
    T = batch * seq
    xp = x_prompt.reshape(T, D_MODEL)
    q, k, v, kb, vb, u, ga, gs, uc = _proj(xp, mods_p[1], mods_p[0], _rope_tables(jnp.arange(seq)), w_in_b,
                                           tm=tm, rows_per_mod=seq, table_blocks=seq // tm, chunk_u=True)
    o = _prompt_attention(q, kb, vb, lams, subw, batch=batch, seq=seq, tq=256, tk=512)
    yc, hf_re, hf_im = _prompt_ssm(uc, prep, batch=batch, seq=seq)
    yp = _merge(xp, o, ga, gs, yc, u, (mods_p[2], mods_p[4], mods_p[3], mods_p[5]), merge_params,
                tm=256, rows_per_mod=seq)

    xs = x_sample.reshape(dec_b, D_MODEL)
    pos_s = jnp.full((dec_b,), past_len, jnp.int32)
    qs, ks, vs, _, _, us, gas, gss = _proj(xs, mods_s[1], mods_s[0], _rope_tables(pos_s), w_in_b,
                                           tm=dec_b, rows_per_mod=dec_b, table_blocks=1, chunk_u=False)
    os_ = _decode_attention(qs, ks, vs, cache_k, cache_v, page_table, lams, subw, l).astype(BF16)
    hs_re, hs_im, y_s = _sample_ssm(us, state_ssm_re[l], state_ssm_im[l], prep, ssm_c_re[l], ssm_c_im[l])
    ys = _merge(xs, os_, gas, gss, y_s, us, (mods_s[2], mods_s[4], mods_s[3], mods_s[5]), merge_params,
                tm=dec_b, rows_per_mod=dec_b)

    st = (SSM_GROUPS, SSM_STATE)
    return (yp.reshape(batch, seq, D_MODEL), ys.reshape(dec_b, 1, D_MODEL),
            k.reshape(1, batch, seq, N_KV_HEADS, QK_DIM), v.reshape(1, batch, seq, N_KV_HEADS, V_DIM),
            hf_re.reshape(1, batch, *st), hf_im.reshape(1, batch, *st),
            ks.reshape(1, dec_b, 1, N_KV_HEADS, QK_DIM), vs.reshape(1, dec_b, 1, N_KV_HEADS, V_DIM),
            hs_re.reshape(1, dec_b, *st), hs_im.reshape(1, dec_b, *st))
```

```python
import functools
import math

import jax
import jax.numpy as jnp
from jax import lax
from jax.experimental import pallas as pl
from jax.experimental.pallas import tpu as pltpu

F32 = jnp.float32
BF16 = jnp.bfloat16

D_MODEL = 1024
N_HEADS = 8
N_KV_HEADS = 4
GQA_GROUP = N_HEADS // N_KV_HEADS
HD = 64
QK_DIM = 2 * HD
V_DIM = 2 * HD
ROT_DIM = HD // 4
ROPE_THETA = 500000.0
SSM_GROUP = 16
SSM_GROUPS = D_MODEL // SSM_GROUP
SSM_STATE = 64
SSM_STATES = SSM_GROUPS * SSM_STATE
D_FF = 4 * D_MODEL
DEPTH = 1
ALPHA = (2.0 * DEPTH) ** 0.25
LN_EPS = 1e-5
LAM_INIT = 0.8 - 0.6 * math.exp(-0.3 * 0)
Q_COLS = N_HEADS * QK_DIM
K_COLS = N_KV_HEADS * QK_DIM
V_COLS = N_KV_HEADS * V_DIM
IN_COLS = Q_COLS + K_COLS + V_COLS + 3 * D_MODEL

LANES = 128
SSM_CHUNK = 16
CHUNK_COLS = SSM_CHUNK * SSM_GROUP
PAIR_STATES = 2 * SSM_STATE
VMEM_LIMIT = 48 * 1024 * 1024

_NT = (((1,), (1,)), ((), ()))
QK_SCALE = math.log2(math.e) / math.sqrt(HD)


def _cparams(n_axes):
    return pltpu.CompilerParams(dimension_semantics=("arbitrary",) * n_axes, vmem_limit_bytes=VMEM_LIMIT)


def _const_spec(shape):
    nd = len(shape)
    return pl.BlockSpec(shape, lambda *_: (0,) * nd, pipeline_mode=pl.Buffered(1))


def _layer_norm(x):
    mu = jnp.mean(x, axis=-1, keepdims=True)
    xc = x - mu
    var = jnp.mean(xc * xc, axis=-1, keepdims=True)
    return xc * lax.rsqrt(var + LN_EPS)


def _ada_kernel(c_ref, w_ref, b_ref, o_ref):
    c = c_ref[...]
    s = c * jax.nn.sigmoid(c)
    o_ref[...] = jnp.dot(s.astype(BF16), w_ref[...], preferred_element_type=F32) + b_ref[...]


def _ada(c_all, w_ada, b_ada):
    rows = c_all.shape[0]
    tn = D_MODEL
    return pl.pallas_call(
        _ada_kernel,
        grid=(6 * D_MODEL // tn,),
        in_specs=[pl.BlockSpec((rows, D_MODEL), lambda j: (0, 0)),
                  pl.BlockSpec((D_MODEL, tn), lambda j: (0, j)),
                  pl.BlockSpec((1, tn), lambda j: (0, j))],
        out_specs=pl.BlockSpec((rows, tn), lambda j: (0, j)),
        out_shape=jax.ShapeDtypeStruct((rows, 6 * D_MODEL), F32),
        compiler_params=_cparams(1),
    )(c_all, w_ada, b_ada)


GROUPS_PER_TILE = LANES // SSM_GROUP
CHUNKS_PER_TILE = LANES // SSM_GROUP


def _lane_group():
    return lax.broadcasted_iota(jnp.int32, (1, LANES), 1) // SSM_GROUP


def _to_chunk_layout(u_s, uc_ref):
    n_chunks = u_s.shape[1] // SSM_CHUNK
    lane_grp = _lane_group()
    for jb in range(SSM_GROUPS // GROUPS_PER_TILE):
        for half in range(SSM_CHUNK // CHUNKS_PER_TILE):
            acc = [None] * GROUPS_PER_TILE
            for t8 in range(CHUNKS_PER_TILE):
                src = u_s[jb, pl.ds(half * CHUNKS_PER_TILE + t8, n_chunks, stride=SSM_CHUNK), :]
                for gl in range(GROUPS_PER_TILE):
                    shift = ((t8 - gl) * SSM_GROUP) % LANES
                    r = pltpu.roll(src, shift, 1) if shift else src
                    acc[gl] = r if t8 == 0 else jnp.where(lane_grp == t8, r, acc[gl])
            for gl in range(GROUPS_PER_TILE):
                uc_ref[jb * GROUPS_PER_TILE + gl, :, half * LANES:(half + 1) * LANES] = acc[gl].astype(BF16)


def _from_chunk_layout(yc_ref, y_s):
    n_chunks = yc_ref.shape[1]
    lane_grp = _lane_group()
    for jb in range(SSM_GROUPS // GROUPS_PER_TILE):
        acc = [None] * SSM_CHUNK
        for gl in range(GROUPS_PER_TILE):
            g = jb * GROUPS_PER_TILE + gl
            for half in range(SSM_CHUNK // CHUNKS_PER_TILE):
                c0 = (g % 2) * CHUNK_COLS + half * LANES
                src = yc_ref[g // 2, :, c0:c0 + LANES].astype(F32)
                for t8 in range(CHUNKS_PER_TILE):
                    tl = half * CHUNKS_PER_TILE + t8
                    shift = ((gl - t8) * SSM_GROUP) % LANES
                    r = pltpu.roll(src, shift, 1) if shift else src
                    acc[tl] = r if gl == 0 else jnp.where(lane_grp == gl, r, acc[tl])
        for tl in range(SSM_CHUNK):
            y_s[jb, pl.ds(tl, n_chunks, stride=SSM_CHUNK), :] = acc[tl]


def _proj_kernel(x_ref, sc_ref, sh_ref, cos_ref, sa_ref, sb_ref, w_ref,
                 q_ref, k_ref, v_ref, kb_ref, vb_ref, u_ref, ga_ref, gs_ref, *maybe_chunked):
    h = (_layer_norm(x_ref[...]) * (1.0 + sc_ref[0]) + sh_ref[0]).astype(BF16)
    cos_t, sin_a, sin_b = cos_ref[...], sa_ref[...], sb_ref[...]

    def rope(t):
        up = pltpu.roll(t, LANES - ROT_DIM // 2, 1)
        dn = pltpu.roll(t, ROT_DIM // 2, 1)
        return t * cos_t + up * sin_a + dn * sin_b

    def mm(c0, c1):
        return jnp.dot(h, w_ref[:, c0:c1], preferred_element_type=F32)

    q = mm(0, Q_COLS)
    for s in range(Q_COLS // LANES):
        sl = slice(s * LANES, (s + 1) * LANES)
        q_ref[:, sl] = (rope(q[:, sl]) * QK_SCALE).astype(BF16)
    tm = x_ref.shape[0]
    k = mm(Q_COLS, Q_COLS + K_COLS)
    c0 = Q_COLS + K_COLS
    v = mm(c0, c0 + V_COLS)
    vb_ref[...] = v.astype(BF16)
    for n in range(N_KV_HEADS):
        sl = slice(n * LANES, (n + 1) * LANES)
        kr = rope(k[:, sl])
        kb_ref[:, sl] = kr.astype(BF16)
        k_ref[pl.ds(n, tm, stride=N_KV_HEADS), :] = kr
        v_ref[pl.ds(n, tm, stride=N_KV_HEADS), :] = v[:, sl]
    c0 += V_COLS
    u = mm(c0, c0 + D_MODEL)
    u_ref[...] = u
    c0 += D_MODEL
    ga_ref[...] = mm(c0, c0 + D_MODEL).astype(BF16)
    c0 += D_MODEL
    gs_ref[...] = mm(c0, c0 + D_MODEL).astype(BF16)
    if maybe_chunked:
        uc_ref, u_s = maybe_chunked
        for jb in range(D_MODEL // LANES):
            u_s[jb] = u[:, jb * LANES:(jb + 1) * LANES]
        _to_chunk_layout(u_s, uc_ref)


def _proj(x, sc, sh, tables, w_in, *, tm, rows_per_mod, table_blocks, chunk_u):
    T = x.shape[0]
    mod_rows = sc.shape[1]
    mod_spec = pl.BlockSpec((1, mod_rows, D_MODEL), lambda i: (i * tm // rows_per_mod, 0, 0))
    tab_spec = pl.BlockSpec((tm, LANES), lambda i: (i % table_blocks, 0))
    row = lambda w: pl.BlockSpec((tm, w), lambda i: (i, 0))
    outs = [(Q_COLS, BF16), (K_COLS, F32), (V_COLS, F32), (K_COLS, BF16), (V_COLS, BF16),
            (D_MODEL, F32), (D_MODEL, BF16), (D_MODEL, BF16)]
    out_specs = [row(w) for w, _ in outs]
    out_shape = [jax.ShapeDtypeStruct((T, w), dt) for w, dt in outs]
    for idx in (1, 2):
        out_specs[idx] = pl.BlockSpec((tm * N_KV_HEADS, QK_DIM), lambda i: (i, 0))
        out_shape[idx] = jax.ShapeDtypeStruct((T * N_KV_HEADS, QK_DIM), F32)
    if chunk_u:
        out_specs.append(pl.BlockSpec((SSM_GROUPS, tm // SSM_CHUNK, CHUNK_COLS), lambda i: (0, i, 0)))
        out_shape.append(jax.ShapeDtypeStruct((SSM_GROUPS, T // SSM_CHUNK, CHUNK_COLS), BF16))
    return pl.pallas_call(
        _proj_kernel,
        grid=(T // tm,),
        in_specs=[row(D_MODEL), mod_spec, mod_spec, tab_spec, tab_spec, tab_spec,
                  _const_spec((D_MODEL, IN_COLS))],
        out_specs=out_specs,
        out_shape=out_shape,
        scratch_shapes=[pltpu.VMEM((D_MODEL // LANES, tm, LANES), F32)] if chunk_u else [],
        compiler_params=_cparams(1),
    )(x, sc, sh, *tables, w_in)


def _rope_tables(pos):
    inv = ROPE_THETA ** (-jnp.arange(0, ROT_DIM, 2, dtype=F32) / ROT_DIM)
    ang = pos.astype(F32)[:, None] * inv[None, :]
    cos, sin = jnp.cos(ang), jnp.sin(ang)
    n, half = pos.shape[0], ROT_DIM // 2
    ones = jnp.ones((n, HD - ROT_DIM), F32)
    zeros = lambda w: jnp.zeros((n, w), F32)
    cos_t = jnp.concatenate([cos, cos, ones], axis=-1)
    sin_a = jnp.concatenate([-sin, zeros(HD - half)], axis=-1)
    sin_b = jnp.concatenate([zeros(half), sin, zeros(HD - ROT_DIM)], axis=-1)
    return tuple(jnp.tile(t, (1, LANES // HD)) for t in (cos_t, sin_a, sin_b))


def _lambda(lq1, lk1, lq2, lk2):
    return (jnp.exp(jnp.sum(lq1[...] * lk1[...], axis=-1, keepdims=True))
            - jnp.exp(jnp.sum(lq2[...] * lk2[...], axis=-1, keepdims=True)) + LAM_INIT)


def _sub_norm(o, subw):
    o = o * lax.rsqrt(jnp.mean(o * o, axis=-1, keepdims=True) + LN_EPS)
    return o * subw * (1.0 - LAM_INIT)


N_MAPS = 2 * GQA_GROUP


def _attn_kernel(lq1, lk1, lq2, lk2, subw_ref, q_ref, k_ref, v_ref, o_ref,
                 q4_s, s_s, mpart_s, m_s, v1_s, acc_s, *, tq, tk):
    qi = pl.program_id(2)
    rows = N_MAPS * tq

    @pl.when(qi == 0)
    def _():
        v1_s[:, :V_DIM] = v_ref[...]
        v1_s[:, V_DIM:] = jnp.ones((v1_s.shape[0], V_DIM), BF16)

    comp0 = lax.broadcasted_iota(jnp.int32, (1, QK_DIM), 1) < HD
    for g in range(GQA_GROUP):
        qg = q_ref[:, g * QK_DIM:(g + 1) * QK_DIM].astype(F32)
        q4_s[(2 * g) * tq:(2 * g + 1) * tq, :] = jnp.where(comp0, qg, 0.0).astype(BF16)
        q4_s[(2 * g + 1) * tq:(2 * g + 2) * tq, :] = jnp.where(comp0, 0.0, qg).astype(BF16)
    n_blk = ((qi + 1) * tq - 1) // tk + 1
    mpart_s[...] = jnp.full(mpart_s.shape, -jnp.inf, F32)

    def scores(ki, masked):
        start = pl.multiple_of(ki * tk, tk)
        s = lax.dot_general(q4_s[...], k_ref[pl.ds(start, tk), :], _NT, preferred_element_type=F32)
        if masked:
            q_pos = qi * tq + jnp.bitwise_and(lax.broadcasted_iota(jnp.int32, (rows, tk), 0), tq - 1)
            k_pos = start + lax.broadcasted_iota(jnp.int32, (rows, tk), 1)
            s = jnp.where(q_pos >= k_pos, s, -jnp.inf)
        s_s[ki] = s
        m = mpart_s[...]
        for j in range(tk // LANES):
            m = jnp.maximum(m, s[:, j * LANES:(j + 1) * LANES])
        mpart_s[...] = m

    def pairs(fn, n):
        def two(i, carry):
            fn(2 * i)
            fn(2 * i + 1)
            return carry

        lax.fori_loop(0, n // 2, two, 0)

        @pl.when(n % 2 == 1)
        def _():
            fn(n - 1)

    pairs(lambda ki: scores(ki, False), n_blk - 1)
    scores(n_blk - 1, True)

    m_s[...] = jnp.broadcast_to(jnp.max(mpart_s[...], axis=-1, keepdims=True), m_s.shape)
    acc_s[...] = jnp.zeros(acc_s.shape, F32)

    def weights(ki):
        start = pl.multiple_of(ki * tk, tk)
        s = s_s[ki]
        m = m_s[...]
        p = jnp.concatenate([jnp.exp2((s[:, j * LANES:(j + 1) * LANES] - m).astype(BF16))
                             for j in range(tk // LANES)], axis=-1)
        acc_s[...] += jnp.dot(p, v1_s[pl.ds(start, tk), :], preferred_element_type=F32)

    pairs(weights, n_blk)

    lam = _lambda(lq1, lk1, lq2, lk2)
    o = acc_s[:, :V_DIM] / acc_s[:, V_DIM:V_DIM + 1]
    for g in range(GQA_GROUP):
        d = o[(2 * g) * tq:(2 * g + 1) * tq] - lam * o[(2 * g + 1) * tq:(2 * g + 2) * tq]
        o_ref[:, g * V_DIM:(g + 1) * V_DIM] = _sub_norm(d, subw_ref[...]).astype(BF16)


def _prompt_attention(q, kb, vb, lams, subw, *, batch, seq, tq, tk):
    assert tk % tq == 0 and tq & (tq - 1) == 0
    T = batch * seq
    nq = seq // tq
    rows = N_MAPS * tq
    small = pl.BlockSpec((1, HD), lambda b, n, i: (0, 0))
    return pl.pallas_call(
        functools.partial(_attn_kernel, tq=tq, tk=tk),
        grid=(batch, N_KV_HEADS, nq),
        in_specs=[small, small, small, small,
                  pl.BlockSpec((1, V_DIM), lambda b, n, i: (0, 0)),
                  pl.BlockSpec((tq, GQA_GROUP * QK_DIM), lambda b, n, i: (b * nq + i, n)),
                  pl.BlockSpec((seq, QK_DIM), lambda b, n, i: (b, n)),
                  pl.BlockSpec((seq, V_DIM), lambda b, n, i: (b, n))],
        out_specs=pl.BlockSpec((tq, GQA_GROUP * V_DIM), lambda b, n, i: (b * nq + i, n)),
        out_shape=jax.ShapeDtypeStruct((T, N_HEADS * V_DIM), BF16),
        scratch_shapes=[pltpu.VMEM((rows, QK_DIM), BF16),
                        pltpu.VMEM((seq // tk, rows, tk), F32),
                        pltpu.VMEM((rows, LANES), F32),
                        pltpu.VMEM((rows, LANES), F32),
                        pltpu.VMEM((seq, 2 * V_DIM), BF16),
                        pltpu.VMEM((rows, 2 * V_DIM), F32)],
        compiler_params=_cparams(3),
    )(*lams, subw, q, kb, vb)


DEC_PAGES_PER_STEP = 16
DEC_ROWS = N_KV_HEADS * N_MAPS


def _decode_kernel(pt_ref, lq1, lk1, lq2, lk2, subw_ref, qa_ref, knew_ref, vnew_ref, *rest):
    del pt_ref
    npg = DEC_PAGES_PER_STEP
    k_refs, v_refs = rest[:npg], rest[npg:2 * npg]
    o_ref, m_s, l_s, acc_s = rest[2 * npg:]
    c = pl.program_id(1)
    qa = qa_ref[0]
    page_rows = k_refs[0].shape[0]

    @pl.when(c == 0)
    def _():
        m_s[...] = jnp.sum(qa.astype(F32) * knew_ref[0], axis=-1, keepdims=True)
        l_s[...] = jnp.ones(l_s.shape, F32)
        acc_s[...] = vnew_ref[0]

    own_head = (jnp.bitwise_and(lax.broadcasted_iota(jnp.int32, (DEC_ROWS, page_rows), 1), N_KV_HEADS - 1)
                == lax.broadcasted_iota(jnp.int32, (DEC_ROWS, page_rows), 0) // N_MAPS)
    s = jnp.concatenate(
        [jnp.where(own_head, lax.dot_general(qa, k_refs[j][...].astype(BF16), _NT, preferred_element_type=F32),
                   -jnp.inf) for j in range(npg)], axis=-1)
    m_prev = m_s[...]
    m_new = jnp.maximum(m_prev, jnp.max(s, axis=-1, keepdims=True))
    alpha = jnp.exp2(m_prev - m_new)
    p = jnp.exp2(s - m_new)
    l_s[...] = alpha * l_s[...] + jnp.sum(p, axis=-1, keepdims=True)
    pv = jnp.dot(p[:, :page_rows].astype(BF16), v_refs[0][...].astype(BF16), preferred_element_type=F32)
    for j in range(1, npg):
        pv += jnp.dot(p[:, j * page_rows:(j + 1) * page_rows].astype(BF16), v_refs[j][...].astype(BF16),
                      preferred_element_type=F32)
    acc_s[...] = alpha * acc_s[...] + pv
    m_s[...] = m_new

    @pl.when(c == pl.num_programs(1) - 1)
    def _():
        lam = _lambda(lq1, lk1, lq2, lk2)
        o = acc_s[...] / l_s[...]
        for n in range(N_KV_HEADS):
            r = n * N_MAPS
            d = o[r:r + GQA_GROUP] - lam * o[r + GQA_GROUP:r + 2 * GQA_GROUP]
            o_ref[0, n * GQA_GROUP:(n + 1) * GQA_GROUP, :] = _sub_norm(d, subw_ref[...])


def _decode_attention(q, k_new, v_new, cache_k, cache_v, page_table, lams, subw, layer):
    bsz = q.shape[0]
    page = cache_k.shape[2]
    n_pages = page_table.shape[1]
    npg = DEC_PAGES_PER_STEP
    q5 = q.reshape(bsz, N_KV_HEADS, GQA_GROUP, 2, HD)
    qa = jnp.einsum('bngcd,ce->bncged', q5, jnp.eye(2, dtype=q.dtype)).reshape(bsz, DEC_ROWS, QK_DIM)
    per_row = lambda t: jnp.repeat(t.reshape(bsz, N_KV_HEADS, QK_DIM), N_MAPS, axis=1)
    small = pl.BlockSpec((1, HD), lambda b, c, pt: (0, 0))
    rows_spec = pl.BlockSpec((1, DEC_ROWS, QK_DIM), lambda b, c, pt: (b, 0, 0))
    n_pool = cache_k.shape[1]
    cache_k = cache_k.reshape(-1, page * N_KV_HEADS, QK_DIM)
    cache_v = cache_v.reshape(-1, page * N_KV_HEADS, V_DIM)
    page_specs = [pl.BlockSpec((None, page * N_KV_HEADS, QK_DIM),
                               functools.partial(lambda b, c, pt, j: (layer * n_pool + pt[b, c * npg + j], 0, 0), j=j))
                  for j in range(npg)]
    grid_spec = pltpu.PrefetchScalarGridSpec(
        num_scalar_prefetch=1,
        grid=(bsz, n_pages // npg),
        in_specs=[small, small, small, small,
                  pl.BlockSpec((1, V_DIM), lambda b, c, pt: (0, 0)),
                  rows_spec, rows_spec, rows_spec] + page_specs + page_specs,
        out_specs=pl.BlockSpec((1, N_HEADS, V_DIM), lambda b, c, pt: (b, 0, 0)),
        scratch_shapes=[pltpu.VMEM((DEC_ROWS, 1), F32), pltpu.VMEM((DEC_ROWS, 1), F32),
                        pltpu.VMEM((DEC_ROWS, V_DIM), F32)])
    o = pl.pallas_call(
        _decode_kernel,
        grid_spec=grid_spec,
        out_shape=jax.ShapeDtypeStruct((bsz, N_HEADS, V_DIM), F32),
        compiler_params=_cparams(2),
    )(page_table, *lams, subw, qa, per_row(k_new), per_row(v_new), *([cache_k] * npg), *([cache_v] * npg))
    return o.reshape(bsz, N_HEADS * V_DIM)


N_POW = 24


def _ssm_prep_kernel(lr_ref, li_ref, ldt_ref, cre_ref, cim_ref, bre_ref, bim_ref,
                     kern_ref, wsr_ref, wsi_ref, gre_ref, gim_ref, a1r_ref, a1i_ref, acr_ref, aci_ref,
                     bbr_ref, bbi_ref):
    lr, li = lr_ref[0], li_ref[0]
    dt = jnp.exp(ldt_ref[0])
    ell = lax.broadcasted_iota(jnp.int32, (N_POW, 1), 0).astype(F32)
    mag = jnp.exp(ell * (lr * dt))
    ph = ell * (li * dt)
    p_re, p_im = mag * jnp.cos(ph), mag * jnp.sin(ph)
    a_re, a_im = p_re[1:2], p_im[1:2]
    a1r_ref[0], a1i_ref[0] = a_re, a_im
    acr_ref[0], aci_ref[0] = p_re[SSM_CHUNK:SSM_CHUNK + 1], p_im[SSM_CHUNK:SSM_CHUNK + 1]
    den = lr * lr + li * li
    nr, ni = a_re - 1.0, a_im
    co_re = (nr * lr + ni * li) / den
    co_im = (ni * lr - nr * li) / den
    b_re, b_im = bre_ref[0], bim_ref[0]
    bb_re = co_re * b_re - co_im * b_im
    bb_im = co_re * b_im + co_im * b_re
    bbr_ref[0], bbi_ref[0] = bb_re, bb_im
    c_re, c_im = cre_ref[0], cim_ref[0]
    for l in range(SSM_CHUNK + 1):
        rows = slice(l * SSM_GROUP, (l + 1) * SSM_GROUP)
        gre_ref[0, rows, :] = c_re * p_re[l:l + 1] - c_im * p_im[l:l + 1]
        gim_ref[0, rows, :] = c_re * p_im[l:l + 1] + c_im * p_re[l:l + 1]
    for s in range(SSM_CHUNK):
        rows = slice(s * SSM_GROUP, (s + 1) * SSM_GROUP)
        e = SSM_CHUNK - 1 - s
        wsr_ref[0, rows, :] = p_re[e:e + 1] * bb_re - p_im[e:e + 1] * bb_im
        wsi_ref[0, rows, :] = p_re[e:e + 1] * bb_im + p_im[e:e + 1] * bb_re
    g_re = gre_ref[0, :CHUNK_COLS, :]
    g_im = gim_ref[0, :CHUNK_COLS, :]
    hi = lax.Precision.HIGHEST
    kern_ref[0] = (lax.dot_general(g_re, bb_re, _NT, precision=hi, preferred_element_type=F32)
                   - lax.dot_general(g_im, bb_im, _NT, precision=hi, preferred_element_type=F32))


def _ssm_prep(lam_re, lam_im, log_dt, b_re, b_im, c_re, c_im):
    G, P, C = SSM_GROUPS, SSM_STATE, SSM_GROUP
    g_rows = (SSM_CHUNK + 1) * C
    vec = pl.BlockSpec((1, 1, P), lambda g: (g, 0, 0))
    mat = pl.BlockSpec((1, C, P), lambda g: (g, 0, 0))
    out_defs = [((CHUNK_COLS, C), None), ((CHUNK_COLS, P), None), ((CHUNK_COLS, P), None),
                ((g_rows, P), None), ((g_rows, P), None),
                ((1, P), None), ((1, P), None), ((1, P), None), ((1, P), None), ((C, P), None), ((C, P), None)]
    return pl.pallas_call(
        _ssm_prep_kernel,
        grid=(G,),
        in_specs=[vec, vec, pl.BlockSpec((1, 1, 1), lambda g: (g, 0, 0)), mat, mat, mat, mat],
        out_specs=[pl.BlockSpec((1,) + s, lambda g: (g, 0, 0)) for s, _ in out_defs],
        out_shape=[jax.ShapeDtypeStruct((G,) + s, F32) for s, _ in out_defs],
        compiler_params=_cparams(1),
    )(lam_re.reshape(G, 1, P), lam_im.reshape(G, 1, P), log_dt.reshape(G, 1, 1),
      c_re, c_im, jnp.swapaxes(b_re, 1, 2), jnp.swapaxes(b_im, 1, 2))


def _pair_block_diag(w):
    g, a, b = w.shape
    w2 = w.reshape(g // 2, 2, a, b)
    return jnp.einsum('qjab,jk->qjakb', w2, jnp.eye(2, dtype=w.dtype)).reshape(g // 2, 2 * a, 2 * b)


def _tile_block_diag(w, per_tile):
    g, a, b = w.shape
    w2 = w.reshape(g // per_tile, per_tile, a, b)
    eye = jnp.eye(per_tile, dtype=w.dtype)
    return jnp.einsum('qjab,jk->qjakb', w2, eye).reshape(g // per_tile, per_tile * a, per_tile * b)


def _ssm_state_kernel(uc_ref, wsr_ref, wsi_ref, sre_ref, sim_ref):
    def both(w_ref):
        return jnp.concatenate([jnp.dot(uc_ref[j], w_ref[j], preferred_element_type=F32) for j in range(2)], axis=-1)
    sre_ref[...] = both(wsr_ref)
    sim_ref[...] = both(wsi_ref)


def _ssm_scan_kernel(ar_ref, ai_ref, sre_ref, sim_ref, hpr_ref, hpi_ref, hfr_ref, hfi_ref):
    ar, ai = ar_ref[...], ai_ref[...]
    batch, n_chunks, width = sre_ref.shape

    def body(k, carry):
        nxt = []
        for b in range(batch):
            hr, hi = carry[2 * b], carry[2 * b + 1]
            row = (b, pl.ds(k, 1), slice(None))
            hpr_ref[row] = hr
            hpi_ref[row] = hi
            nxt += [ar * hr - ai * hi + sre_ref[row], ar * hi + ai * hr + sim_ref[row]]
        return tuple(nxt)

    zero = jnp.zeros((1, width), F32)
    fin = lax.fori_loop(0, n_chunks, body, (zero,) * (2 * batch))
    for b in range(batch):
        hfr_ref[b:b + 1, :] = fin[2 * b]
        hfi_ref[b:b + 1, :] = fin[2 * b + 1]


def _ssm_out_kernel(uc_ref, toep_ref, hpr_ref, hpi_ref, wcr_ref, wci_ref, y_ref):
    y = jnp.concatenate([jnp.dot(uc_ref[j], toep_ref[j], preferred_element_type=F32) for j in range(2)], axis=-1)
    y += jnp.dot(hpr_ref[...].astype(BF16), wcr_ref[0], preferred_element_type=F32)
    y += jnp.dot(hpi_ref[...].astype(BF16), wci_ref[0], preferred_element_type=F32)
    y_ref[0] = y.astype(BF16)


def _prompt_ssm(uc, prep, *, batch, seq):
    kern, wsr, wsi, g_re, g_im, _, _, ac_re, ac_im, _, _ = prep
    G, P, C, TL = SSM_GROUPS, SSM_STATE, SSM_GROUP, SSM_CHUNK
    nk = seq // TL
    rows = nk * batch
    k4 = kern.reshape(G, TL, C, C)
    lag = jnp.arange(TL)[None, :] - jnp.arange(TL)[:, None]
    toep = jnp.where((lag >= 0)[None, :, :, None, None], k4[:, jnp.clip(lag, 0, TL - 1)], 0.0)
    toep = toep.transpose(0, 1, 4, 2, 3).reshape(G, CHUNK_COLS, CHUNK_COLS).astype(BF16)
    wc_re = g_re.reshape(G, TL + 1, C, P)[:, 1:].transpose(0, 3, 1, 2).reshape(G, P, CHUNK_COLS)
    wc_im = -g_im.reshape(G, TL + 1, C, P)[:, 1:].transpose(0, 3, 1, 2).reshape(G, P, CHUNK_COLS)
    wc_re, wc_im = _pair_block_diag(wc_re).astype(BF16), _pair_block_diag(wc_im).astype(BF16)

    pair3 = lambda a, b: pl.BlockSpec((2, a, b), lambda q: (q, 0, 0))
    s_spec = pl.BlockSpec((rows, PAIR_STATES), lambda q: (0, q))
    s_re, s_im = pl.pallas_call(
        _ssm_state_kernel,
        grid=(G // 2,),
        in_specs=[pair3(rows, CHUNK_COLS), pair3(CHUNK_COLS, P), pair3(CHUNK_COLS, P)],
        out_specs=[s_spec, s_spec],
        out_shape=[jax.ShapeDtypeStruct((rows, G * P), F32)] * 2,
        compiler_params=_cparams(1),
    )(uc, wsr.astype(BF16), wsi.astype(BF16))

    wb = 512
    a_spec = pl.BlockSpec((1, wb), lambda j: (0, j))
    seq_spec = pl.BlockSpec((batch, nk, wb), lambda j: (0, 0, j))
    fin_spec = pl.BlockSpec((batch, wb), lambda j: (0, j))
    hp_re, hp_im, hf_re, hf_im = pl.pallas_call(
        _ssm_scan_kernel,
        grid=(G * P // wb,),
        in_specs=[a_spec, a_spec, seq_spec, seq_spec],
        out_specs=[seq_spec, seq_spec, fin_spec, fin_spec],
        out_shape=[jax.ShapeDtypeStruct((batch, nk, G * P), F32)] * 2 + [jax.ShapeDtypeStruct((batch, G * P), F32)] * 2,
        compiler_params=_cparams(1),
    )(ac_re.reshape(1, G * P), ac_im.reshape(1, G * P),
      s_re.reshape(batch, nk, G * P), s_im.reshape(batch, nk, G * P))

    yc = pl.pallas_call(
        _ssm_out_kernel,
        grid=(G // 2,),
        in_specs=[pair3(rows, CHUNK_COLS), pair3(CHUNK_COLS, CHUNK_COLS), s_spec, s_spec,
                  pl.BlockSpec((1, PAIR_STATES, 2 * CHUNK_COLS), lambda q: (q, 0, 0)),
                  pl.BlockSpec((1, PAIR_STATES, 2 * CHUNK_COLS), lambda q: (q, 0, 0))],
        out_specs=pl.BlockSpec((1, rows, 2 * CHUNK_COLS), lambda q: (q, 0, 0)),
        out_shape=jax.ShapeDtypeStruct((G // 2, rows, 2 * CHUNK_COLS), BF16),
        compiler_params=_cparams(1),
    )(uc, toep, hp_re.reshape(rows, G * P), hp_im.reshape(rows, G * P), wc_re, wc_im)
    return yc, hf_re, hf_im


SSM_TILE_GROUPS = 16


def _ssm_step_kernel(u_ref, h0r_ref, h0i_ref, ar_ref, ai_ref, bdr_ref, bdi_ref, cdr_ref, cdi_ref,
                     hr_ref, hi_ref, y_ref):
    ch = SSM_TILE_GROUPS * SSM_GROUP
    st = SSM_TILE_GROUPS * SSM_STATE
    for t in range(SSM_GROUPS // SSM_TILE_GROUPS):
        cs, ss = slice(t * ch, (t + 1) * ch), slice(t * st, (t + 1) * st)
        ub = u_ref[:, cs].astype(BF16)
        ar, ai, h0r, h0i = ar_ref[:, ss], ai_ref[:, ss], h0r_ref[:, ss], h0i_ref[:, ss]
        hr = ar * h0r - ai * h0i + jnp.dot(ub, bdr_ref[t], preferred_element_type=F32)
        hi = ar * h0i + ai * h0r + jnp.dot(ub, bdi_ref[t], preferred_element_type=F32)
        hr_ref[:, ss] = hr
        hi_ref[:, ss] = hi
        y_ref[:, cs] = (jnp.dot(hr.astype(BF16), cdr_ref[t], preferred_element_type=F32)
                        + jnp.dot(hi.astype(BF16), cdi_ref[t], preferred_element_type=F32))


def _sample_ssm(u, h0_re, h0_im, prep, c_re, c_im):
    _, _, _, _, _, a_re, a_im, _, _, bb_re, bb_im = prep
    bsz = u.shape[0]
    n_st = SSM_STATES
    bd_re = _tile_block_diag(bb_re, SSM_TILE_GROUPS).astype(BF16)
    bd_im = _tile_block_diag(bb_im, SSM_TILE_GROUPS).astype(BF16)
    cd_re = _tile_block_diag(jnp.swapaxes(c_re, 1, 2), SSM_TILE_GROUPS).astype(BF16)
    cd_im = _tile_block_diag(-jnp.swapaxes(c_im, 1, 2), SSM_TILE_GROUPS).astype(BF16)
    args = (u, h0_re.reshape(bsz, n_st), h0_im.reshape(bsz, n_st), a_re.reshape(1, n_st), a_im.reshape(1, n_st),
            bd_re, bd_im, cd_re, cd_im)
    return pl.pallas_call(
        _ssm_step_kernel,
        grid=(1,),
        in_specs=[_const_spec(a.shape) for a in args],
        out_specs=[pl.BlockSpec((bsz, n_st), lambda i: (0, 0))] * 2 + [pl.BlockSpec((bsz, D_MODEL), lambda i: (0, 0))],
        out_shape=[jax.ShapeDtypeStruct((bsz, n_st), F32)] * 2 + [jax.ShapeDtypeStruct((bsz, D_MODEL), F32)],
        compiler_params=_cparams(1),
    )(*args)


FF_CHUNK = 1024


def _merge_kernel(x_ref, o_ref, ga_ref, gs_ref, y_ref, u_ref, g1_ref, sc2_ref, sh2_ref, g2_ref,
                  d_ref, wglu_ref, bglu_ref, wo_ref, ln1g_ref, ln1b_ref,
                  wup_ref, bup_ref, wdn_ref, bdn_ref, ln2g_ref, ln2b_ref, out_ref, *maybe_y_s):
    if maybe_y_s:
        y_s = maybe_y_s[0]
        _from_chunk_layout(y_ref, y_s)
        y = jnp.concatenate([y_s[jb] for jb in range(D_MODEL // LANES)], axis=-1)
    else:
        y = y_ref[...]
    yd = y + d_ref[...] * u_ref[...]
    g = jax.nn.gelu(yd, approximate=True)
    z = jnp.dot(g.astype(BF16), wglu_ref[...], preferred_element_type=F32) + bglu_ref[...]
    y_ssm = g * jax.nn.sigmoid(z)
    mixed = (jax.nn.sigmoid(ga_ref[...].astype(F32)) * o_ref[...].astype(F32)
             + jax.nn.sigmoid(gs_ref[...].astype(F32)) * y_ssm)
    t = jnp.dot(mixed.astype(BF16), wo_ref[...], preferred_element_type=F32)
    x1 = _layer_norm(ALPHA * x_ref[...] + (1.0 + g1_ref[0]) * t) * ln1g_ref[...] + ln1b_ref[...]
    h2 = (_layer_norm(x1) * (1.0 + sc2_ref[0]) + sh2_ref[0]).astype(BF16)
    f = jnp.zeros(x1.shape, F32)
    for c in range(D_FF // FF_CHUNK):
        cs = slice(c * FF_CHUNK, (c + 1) * FF_CHUNK)
        a = jnp.maximum(jnp.dot(h2, wup_ref[:, cs], preferred_element_type=F32) + bup_ref[:, cs], 0.0)
        f += jnp.dot((a * a).astype(BF16), wdn_ref[cs, :], preferred_element_type=F32)
    f += bdn_ref[...]
    out_ref[...] = _layer_norm(ALPHA * x1 + (1.0 + g2_ref[0]) * f) * ln2g_ref[...] + ln2b_ref[...]


def _merge(x, o, ga, gs, y, u, mods, params, *, tm, rows_per_mod):
    T = x.shape[0]
    mod_rows = mods[0].shape[1]
    mod_spec = pl.BlockSpec((1, mod_rows, D_MODEL), lambda i: (i * tm // rows_per_mod, 0, 0))
    row = pl.BlockSpec((tm, D_MODEL), lambda i: (i, 0))
    chunked = y.ndim == 3
    y_spec = pl.BlockSpec((SSM_GROUPS // 2, tm // SSM_CHUNK, 2 * CHUNK_COLS), lambda i: (0, i, 0)) if chunked else row
    return pl.pallas_call(
        _merge_kernel,
        grid=(T // tm,),
        in_specs=[row] * 4 + [y_spec, row] + [mod_spec] * 4 + [_const_spec(p.shape) for p in params],
        out_specs=row,
        out_shape=jax.ShapeDtypeStruct((T, D_MODEL), F32),
        scratch_shapes=[pltpu.VMEM((D_MODEL // LANES, tm, LANES), F32)] if chunked else [],
        compiler_params=_cparams(1),
    )(x, o, ga, gs, y, u, *mods, *params)


def _split_mods(ada):
    return [ada[..., i * D_MODEL:(i + 1) * D_MODEL] for i in range(6)]


def kernel(x_prompt, x_sample, cache_k, cache_v, state_ssm_re, state_ssm_im, page_table, c_prompt, c_sample, w_in, lambda_q1, lambda_k1, lambda_q2, lambda_k2, subln_w, ssm_lambda_re, ssm_lambda_im, ssm_log_dt, ssm_b_re, ssm_b_im, ssm_c_re, ssm_c_im, ssm_d, w_glu, b_glu, w_o, w_ada, b_ada, ln1_g, ln1_b, ln2_g, ln2_b, w_up, b_up, w_down, b_down):
    assert w_in.shape[0] == DEPTH and x_sample.shape[1] == 1
    batch, seq, _ = x_prompt.shape
    dec_b = x_sample.shape[0]
    page = cache_k.shape[2]
    past_len = page_table.shape[1] * page
    l = 0
    row2 = lambda a: a[l].reshape(1, -1)

    pad = (-batch) % 8
    c_all = jnp.concatenate([c_prompt, jnp.zeros((pad, D_MODEL), F32), c_sample], axis=0)
    ada = _ada(c_all, w_ada[l].astype(BF16), row2(b_ada))
    mods_p = _split_mods(ada[:batch].reshape(batch, 1, 6 * D_MODEL))
    mods_s = _split_mods(ada[batch + pad:].reshape(1, dec_b, 6 * D_MODEL))

    w_in_b = w_in[l].astype(BF16)
    lams = (row2(lambda_q1), row2(lambda_k1), row2(lambda_q2), row2(lambda_k2))
    subw = row2(subln_w)
    merge_params = (row2(ssm_d), w_glu[l].astype(BF16), row2(b_glu), w_o[l].astype(BF16), row2(ln1_g), row2(ln1_b),
                    w_up[l].astype(BF16), row2(b_up), w_down[l].astype(BF16), row2(b_down), row2(ln2_g), row2(ln2_b))
    prep = _ssm_prep(ssm_lambda_re[l], ssm_lambda_im[l], ssm_log_dt[l], ssm_b_re[l], ssm_b_im[l],
                     ssm_c_re[l], ssm_c_im[l])

    tm = 512
    T = batch * seq
    xp = x_prompt.reshape(T, D_MODEL)
    q, k, v, kb, vb, u, ga, gs, uc = _proj(xp, mods_p[1], mods_p[0], _rope_tables(jnp.arange(seq)), w_in_b,
                                           tm=tm, rows_per_mod=seq, table_blocks=seq // tm, chunk_u=True)
    o = _prompt_attention(q, kb, vb, lams, subw, batch=batch, seq=seq, tq=256, tk=512)
    yc, hf_re, hf_im = _prompt_ssm(uc, prep, batch=batch, seq=seq)
    yp = _merge(xp, o, ga, gs, yc, u, (mods_p[2], mods_p[4], mods_p[3], mods_p[5]), merge_params,
                tm=256, rows_per_mod=seq)

    xs = x_sample.reshape(dec_b, D_MODEL)
    pos_s = jnp.full((dec_b,), past_len, jnp.int32)
    qs, ks, vs, _, _, us, gas, gss = _proj(xs, mods_s[1], mods_s[0], _rope_tables(pos_s), w_in_b,
                                           tm=dec_b, rows_per_mod=dec_b, table_blocks=1, chunk_u=False)
    os_ = _decode_attention(qs, ks, vs, cache_k, cache_v, page_table, lams, subw, l).astype(BF16)
    hs_re, hs_im, y_s = _sample_ssm(us, state_ssm_re[l], state_ssm_im[l], prep, ssm_c_re[l], ssm_c_im[l])
    ys = _merge(xs, os_, gas, gss, y_s, us, (mods_s[2], mods_s[4], mods_s[3], mods_s[5]), merge_params,
                tm=dec_b, rows_per_mod=dec_b)

    st = (SSM_GROUPS, SSM_STATE)
    return (yp.reshape(batch, seq, D_MODEL), ys.reshape(dec_b, 1, D_MODEL),
            k.reshape(1, batch, seq, N_KV_HEADS, QK_DIM), v.reshape(1, batch, seq, N_KV_HEADS, V_DIM),
            hf_re.reshape(1, batch, *st), hf_im.reshape(1, batch, *st),
            ks.reshape(1, dec_b, 1, N_KV_HEADS, QK_DIM), vs.reshape(1, dec_b, 1, N_KV_HEADS, V_DIM),
            hs_re.reshape(1, dec_b, *st), hs_im.reshape(1, dec_b, *st))
```

```python
import functools
import math

import jax
import jax.numpy as jnp
from jax import lax
from jax.experimental import pallas as pl
from jax.experimental.pallas import tpu as pltpu

F32 = jnp.float32
BF16 = jnp.bfloat16

D_MODEL = 1024
N_HEADS = 8
N_KV_HEADS = 4
GQA_GROUP = N_HEADS // N_KV_HEADS
HD = 64
QK_DIM = 2 * HD
V_DIM = 2 * HD
ROT_DIM = HD // 4
ROPE_THETA = 500000.0
SSM_GROUP = 16
SSM_GROUPS = D_MODEL // SSM_GROUP
SSM_STATE = 64
SSM_STATES = SSM_GROUPS * SSM_STATE
D_FF = 4 * D_MODEL
DEPTH = 1
ALPHA = (2.0 * DEPTH) ** 0.25
LN_EPS = 1e-5
LAM_INIT = 0.8 - 0.6 * math.exp(-0.3 * 0)
Q_COLS = N_HEADS * QK_DIM
K_COLS = N_KV_HEADS * QK_DIM
V_COLS = N_KV_HEADS * V_DIM
IN_COLS = Q_COLS + K_COLS + V_COLS + 3 * D_MODEL

LANES = 128
SSM_CHUNK = 16
CHUNK_COLS = SSM_CHUNK * SSM_GROUP
PAIR_STATES = 2 * SSM_STATE
VMEM_LIMIT = 48 * 1024 * 1024

_NT = (((1,), (1,)), ((), ()))
QK_SCALE = math.log2(math.e) / math.sqrt(HD)


def _cparams(n_axes):
    return pltpu.CompilerParams(dimension_semantics=("arbitrary",) * n_axes, vmem_limit_bytes=VMEM_LIMIT)


def _const_spec(shape):
    nd = len(shape)
    return pl.BlockSpec(shape, lambda *_: (0,) * nd, pipeline_mode=pl.Buffered(1))


def _layer_norm(x):
    mu = jnp.mean(x, axis=-1, keepdims=True)
    xc = x - mu
    var = jnp.mean(xc * xc, axis=-1, keepdims=True)
    return xc * lax.rsqrt(var + LN_EPS)


def _ada_kernel(c_ref, w_ref, b_ref, o_ref):
    c = c_ref[...]
    s = c * jax.nn.sigmoid(c)
    o_ref[...] = jnp.dot(s.astype(BF16), w_ref[...].astype(BF16), preferred_element_type=F32) + b_ref[...]


def _ada(c_all, w_ada, b_ada):
    rows = c_all.shape[0]
    tn = D_MODEL
    return pl.pallas_call(
        _ada_kernel,
        grid=(6 * D_MODEL // tn,),
        in_specs=[pl.BlockSpec((rows, D_MODEL), lambda j: (0, 0)),
                  pl.BlockSpec((D_MODEL, tn), lambda j: (0, j)),
                  pl.BlockSpec((1, tn), lambda j: (0, j))],
        out_specs=pl.BlockSpec((rows, tn), lambda j: (0, j)),
        out_shape=jax.ShapeDtypeStruct((rows, 6 * D_MODEL), F32),
        compiler_params=_cparams(1),
    )(c_all, w_ada, b_ada)


GROUPS_PER_TILE = LANES // SSM_GROUP
CHUNKS_PER_TILE = LANES // SSM_GROUP


def _lane_group():
    return lax.broadcasted_iota(jnp.int32, (1, LANES), 1) // SSM_GROUP


def _to_chunk_layout(u_s, uc_ref):
    n_chunks = u_s.shape[1] // SSM_CHUNK
    lane_grp = _lane_group()
    for jb in range(SSM_GROUPS // GROUPS_PER_TILE):
        for half in range(SSM_CHUNK // CHUNKS_PER_TILE):
            acc = [None] * GROUPS_PER_TILE
            for t8 in range(CHUNKS_PER_TILE):
                src = u_s[jb, pl.ds(half * CHUNKS_PER_TILE + t8, n_chunks, stride=SSM_CHUNK), :]
                for gl in range(GROUPS_PER_TILE):
                    shift = ((t8 - gl) * SSM_GROUP) % LANES
                    r = pltpu.roll(src, shift, 1) if shift else src
                    acc[gl] = r if t8 == 0 else jnp.where(lane_grp == t8, r, acc[gl])
            for gl in range(GROUPS_PER_TILE):
                uc_ref[jb * GROUPS_PER_TILE + gl, :, half * LANES:(half + 1) * LANES] = acc[gl].astype(BF16)


def _from_chunk_layout(yc_ref, y_s):
    n_chunks = yc_ref.shape[1]
    lane_grp = _lane_group()
    for jb in range(SSM_GROUPS // GROUPS_PER_TILE):
        acc = [None] * SSM_CHUNK
        for gl in range(GROUPS_PER_TILE):
            g = jb * GROUPS_PER_TILE + gl
            for half in range(SSM_CHUNK // CHUNKS_PER_TILE):
                c0 = (g % 2) * CHUNK_COLS + half * LANES
                src = yc_ref[g // 2, :, c0:c0 + LANES].astype(F32)
                for t8 in range(CHUNKS_PER_TILE):
                    tl = half * CHUNKS_PER_TILE + t8
                    shift = ((gl - t8) * SSM_GROUP) % LANES
                    r = pltpu.roll(src, shift, 1) if shift else src
                    acc[tl] = r if gl == 0 else jnp.where(lane_grp == gl, r, acc[tl])
        for tl in range(SSM_CHUNK):
            y_s[jb, pl.ds(tl, n_chunks, stride=SSM_CHUNK), :] = acc[tl]


def _proj_kernel(x_ref, sc_ref, sh_ref, cos_ref, sa_ref, sb_ref, w_ref,
                 q_ref, k_ref, v_ref, kb_ref, vb_ref, u_ref, ga_ref, gs_ref, *maybe_chunked):
    h = (_layer_norm(x_ref[...]) * (1.0 + sc_ref[0]) + sh_ref[0]).astype(BF16)
    cos_t, sin_a, sin_b = cos_ref[...], sa_ref[...], sb_ref[...]

    def rope(t):
        up = pltpu.roll(t, LANES - ROT_DIM // 2, 1)
        dn = pltpu.roll(t, ROT_DIM // 2, 1)
        return t * cos_t + up * sin_a + dn * sin_b

    def mm(c0, c1):
        return jnp.dot(h, w_ref[:, c0:c1], preferred_element_type=F32)

    q = mm(0, Q_COLS)
    for s in range(Q_COLS // LANES):
        sl = slice(s * LANES, (s + 1) * LANES)
        q_ref[:, sl] = (rope(q[:, sl]) * QK_SCALE).astype(BF16)
    tm = x_ref.shape[0]
    k = mm(Q_COLS, Q_COLS + K_COLS)
    c0 = Q_COLS + K_COLS
    v = mm(c0, c0 + V_COLS)
    vb_ref[...] = v.astype(BF16)
    for n in range(N_KV_HEADS):
        sl = slice(n * LANES, (n + 1) * LANES)
        kr = rope(k[:, sl])
        kb_ref[:, sl] = kr.astype(BF16)
        k_ref[pl.ds(n, tm, stride=N_KV_HEADS), :] = kr
        v_ref[pl.ds(n, tm, stride=N_KV_HEADS), :] = v[:, sl]
    c0 += V_COLS
    u = mm(c0, c0 + D_MODEL)
    u_ref[...] = u
    c0 += D_MODEL
    ga_ref[...] = mm(c0, c0 + D_MODEL).astype(BF16)
    c0 += D_MODEL
    gs_ref[...] = mm(c0, c0 + D_MODEL).astype(BF16)
    if maybe_chunked:
        uc_ref, u_s = maybe_chunked
        for jb in range(D_MODEL // LANES):
            u_s[jb] = u[:, jb * LANES:(jb + 1) * LANES]
        _to_chunk_layout(u_s, uc_ref)


def _proj(x, sc, sh, tables, w_in, *, tm, rows_per_mod, table_blocks, chunk_u):
    T = x.shape[0]
    mod_rows = sc.shape[1]
    mod_spec = pl.BlockSpec((1, mod_rows, D_MODEL), lambda i: (i * tm // rows_per_mod, 0, 0))
    tab_spec = pl.BlockSpec((tm, LANES), lambda i: (i % table_blocks, 0))
    row = lambda w: pl.BlockSpec((tm, w), lambda i: (i, 0))
    outs = [(Q_COLS, BF16), (K_COLS, F32), (V_COLS, F32), (K_COLS, BF16), (V_COLS, BF16),
            (D_MODEL, F32), (D_MODEL, BF16), (D_MODEL, BF16)]
    out_specs = [row(w) for w, _ in outs]
    out_shape = [jax.ShapeDtypeStruct((T, w), dt) for w, dt in outs]
    for idx in (1, 2):
        out_specs[idx] = pl.BlockSpec((tm * N_KV_HEADS, QK_DIM), lambda i: (i, 0))
        out_shape[idx] = jax.ShapeDtypeStruct((T * N_KV_HEADS, QK_DIM), F32)
    if chunk_u:
        out_specs.append(pl.BlockSpec((SSM_GROUPS, tm // SSM_CHUNK, CHUNK_COLS), lambda i: (0, i, 0)))
        out_shape.append(jax.ShapeDtypeStruct((SSM_GROUPS, T // SSM_CHUNK, CHUNK_COLS), BF16))
    return pl.pallas_call(
        _proj_kernel,
        grid=(T // tm,),
        in_specs=[row(D_MODEL), mod_spec, mod_spec, tab_spec, tab_spec, tab_spec,
                  _const_spec((D_MODEL, IN_COLS))],
        out_specs=out_specs,
        out_shape=out_shape,
        scratch_shapes=[pltpu.VMEM((D_MODEL // LANES, tm, LANES), F32)] if chunk_u else [],
        compiler_params=_cparams(1),
    )(x, sc, sh, *tables, w_in)


def _rope_tables(pos):
    inv = ROPE_THETA ** (-jnp.arange(0, ROT_DIM, 2, dtype=F32) / ROT_DIM)
    ang = pos.astype(F32)[:, None] * inv[None, :]
    cos, sin = jnp.cos(ang), jnp.sin(ang)
    n, half = pos.shape[0], ROT_DIM // 2
    ones = jnp.ones((n, HD - ROT_DIM), F32)
    zeros = lambda w: jnp.zeros((n, w), F32)
    cos_t = jnp.concatenate([cos, cos, ones], axis=-1)
    sin_a = jnp.concatenate([-sin, zeros(HD - half)], axis=-1)
    sin_b = jnp.concatenate([zeros(half), sin, zeros(HD - ROT_DIM)], axis=-1)
    return tuple(jnp.tile(t, (1, LANES // HD)) for t in (cos_t, sin_a, sin_b))


def _lambda(lq1, lk1, lq2, lk2):
    return (jnp.exp(jnp.sum(lq1[...] * lk1[...], axis=-1, keepdims=True))
            - jnp.exp(jnp.sum(lq2[...] * lk2[...], axis=-1, keepdims=True)) + LAM_INIT)


def _sub_norm(o, subw):
    o = o * lax.rsqrt(jnp.mean(o * o, axis=-1, keepdims=True) + LN_EPS)
    return o * subw * (1.0 - LAM_INIT)


N_MAPS = 2 * GQA_GROUP


def _attn_kernel(lq1, lk1, lq2, lk2, subw_ref, q_ref, k_ref, v_ref, o_ref,
                 q4_s, s_s, mpart_s, m_s, v1_s, acc_s, *, tq, tk):
    qi = pl.program_id(2)
    rows = N_MAPS * tq

    @pl.when(qi == 0)
    def _():
        v1_s[:, :V_DIM] = v_ref[...]
        v1_s[:, V_DIM:] = jnp.ones((v1_s.shape[0], V_DIM), BF16)

    comp0 = lax.broadcasted_iota(jnp.int32, (1, QK_DIM), 1) < HD
    for g in range(GQA_GROUP):
        qg = q_ref[:, g * QK_DIM:(g + 1) * QK_DIM].astype(F32)
        q4_s[(2 * g) * tq:(2 * g + 1) * tq, :] = jnp.where(comp0, qg, 0.0).astype(BF16)
        q4_s[(2 * g + 1) * tq:(2 * g + 2) * tq, :] = jnp.where(comp0, 0.0, qg).astype(BF16)
    n_blk = ((qi + 1) * tq - 1) // tk + 1
    mpart_s[...] = jnp.full(mpart_s.shape, -jnp.inf, F32)

    def scores(ki, masked):
        start = pl.multiple_of(ki * tk, tk)
        s = lax.dot_general(q4_s[...], k_ref[pl.ds(start, tk), :], _NT, preferred_element_type=F32)
        if masked:
            q_pos = qi * tq + jnp.bitwise_and(lax.broadcasted_iota(jnp.int32, (rows, tk), 0), tq - 1)
            k_pos = start + lax.broadcasted_iota(jnp.int32, (rows, tk), 1)
            s = jnp.where(q_pos >= k_pos, s, -jnp.inf)
        s_s[ki] = s
        m = mpart_s[...]
        for j in range(tk // LANES):
            m = jnp.maximum(m, s[:, j * LANES:(j + 1) * LANES])
        mpart_s[...] = m

    def pairs(fn, n):
        def two(i, carry):
            fn(2 * i)
            fn(2 * i + 1)
            return carry

        lax.fori_loop(0, n // 2, two, 0)

        @pl.when(n % 2 == 1)
        def _():
            fn(n - 1)

    pairs(lambda ki: scores(ki, False), n_blk - 1)
    scores(n_blk - 1, True)

    m_s[...] = jnp.broadcast_to(jnp.max(mpart_s[...], axis=-1, keepdims=True), m_s.shape)
    acc_s[...] = jnp.zeros(acc_s.shape, F32)

    def weights(ki):
        start = pl.multiple_of(ki * tk, tk)
        s = s_s[ki]
        m = m_s[...]
        p = jnp.concatenate([jnp.exp2((s[:, j * LANES:(j + 1) * LANES] - m).astype(BF16))
                             for j in range(tk // LANES)], axis=-1)
        acc_s[...] += jnp.dot(p, v1_s[pl.ds(start, tk), :], preferred_element_type=F32)

    pairs(weights, n_blk)

    lam = _lambda(lq1, lk1, lq2, lk2)
    o = acc_s[:, :V_DIM] / acc_s[:, V_DIM:V_DIM + 1]
    for g in range(GQA_GROUP):
        d = o[(2 * g) * tq:(2 * g + 1) * tq] - lam * o[(2 * g + 1) * tq:(2 * g + 2) * tq]
        o_ref[:, g * V_DIM:(g + 1) * V_DIM] = _sub_norm(d, subw_ref[...]).astype(BF16)


def _prompt_attention(q, kb, vb, lams, subw, *, batch, seq, tq, tk):
    assert tk % tq == 0 and tq & (tq - 1) == 0
    T = batch * seq
    nq = seq // tq
    rows = N_MAPS * tq
    small = pl.BlockSpec((1, HD), lambda b, n, i: (0, 0))
    return pl.pallas_call(
        functools.partial(_attn_kernel, tq=tq, tk=tk),
        grid=(batch, N_KV_HEADS, nq),
        in_specs=[small, small, small, small,
                  pl.BlockSpec((1, V_DIM), lambda b, n, i: (0, 0)),
                  pl.BlockSpec((tq, GQA_GROUP * QK_DIM), lambda b, n, i: (b * nq + i, n)),
                  pl.BlockSpec((seq, QK_DIM), lambda b, n, i: (b, n)),
                  pl.BlockSpec((seq, V_DIM), lambda b, n, i: (b, n))],
        out_specs=pl.BlockSpec((tq, GQA_GROUP * V_DIM), lambda b, n, i: (b * nq + i, n)),
        out_shape=jax.ShapeDtypeStruct((T, N_HEADS * V_DIM), BF16),
        scratch_shapes=[pltpu.VMEM((rows, QK_DIM), BF16),
                        pltpu.VMEM((seq // tk, rows, tk), F32),
                        pltpu.VMEM((rows, LANES), F32),
                        pltpu.VMEM((rows, LANES), F32),
                        pltpu.VMEM((seq, 2 * V_DIM), BF16),
                        pltpu.VMEM((rows, 2 * V_DIM), F32)],
        compiler_params=_cparams(3),
    )(*lams, subw, q, kb, vb)


DEC_PAGES_PER_STEP = 16
DEC_ROWS = N_KV_HEADS * N_MAPS


def _decode_kernel(pt_ref, lq1, lk1, lq2, lk2, subw_ref, qa_ref, knew_ref, vnew_ref, *rest):
    del pt_ref
    npg = DEC_PAGES_PER_STEP
    k_refs, v_refs = rest[:npg], rest[npg:2 * npg]
    o_ref, m_s, l_s, acc_s = rest[2 * npg:]
    c = pl.program_id(1)
    qa = qa_ref[0]
    page_rows = k_refs[0].shape[0]

    @pl.when(c == 0)
    def _():
        m_s[...] = jnp.sum(qa.astype(F32) * knew_ref[0], axis=-1, keepdims=True)
        l_s[...] = jnp.ones(l_s.shape, F32)
        acc_s[...] = vnew_ref[0]

    own_head = (jnp.bitwise_and(lax.broadcasted_iota(jnp.int32, (DEC_ROWS, page_rows), 1), N_KV_HEADS - 1)
                == lax.broadcasted_iota(jnp.int32, (DEC_ROWS, page_rows), 0) // N_MAPS)
    s = jnp.concatenate(
        [jnp.where(own_head, lax.dot_general(qa, k_refs[j][...].astype(BF16), _NT, preferred_element_type=F32),
                   -jnp.inf) for j in range(npg)], axis=-1)
    m_prev = m_s[...]
    m_new = jnp.maximum(m_prev, jnp.max(s, axis=-1, keepdims=True))
    alpha = jnp.exp2(m_prev - m_new)
    p = jnp.exp2(s - m_new)
    l_s[...] = alpha * l_s[...] + jnp.sum(p, axis=-1, keepdims=True)
    pv = jnp.dot(p[:, :page_rows].astype(BF16), v_refs[0][...].astype(BF16), preferred_element_type=F32)
    for j in range(1, npg):
        pv += jnp.dot(p[:, j * page_rows:(j + 1) * page_rows].astype(BF16), v_refs[j][...].astype(BF16),
                      preferred_element_type=F32)
    acc_s[...] = alpha * acc_s[...] + pv
    m_s[...] = m_new

    @pl.when(c == pl.num_programs(1) - 1)
    def _():
        lam = _lambda(lq1, lk1, lq2, lk2)
        o = acc_s[...] / l_s[...]
        for n in range(N_KV_HEADS):
            r = n * N_MAPS
            d = o[r:r + GQA_GROUP] - lam * o[r + GQA_GROUP:r + 2 * GQA_GROUP]
            o_ref[0, n * GQA_GROUP:(n + 1) * GQA_GROUP, :] = _sub_norm(d, subw_ref[...])


def _decode_attention(q, k_new, v_new, cache_k, cache_v, page_table, lams, subw, layer):
    bsz = q.shape[0]
    page = cache_k.shape[2]
    n_pages = page_table.shape[1]
    npg = DEC_PAGES_PER_STEP
    q5 = q.reshape(bsz, N_KV_HEADS, GQA_GROUP, 2, HD)
    qa = jnp.einsum('bngcd,ce->bncged', q5, jnp.eye(2, dtype=q.dtype)).reshape(bsz, DEC_ROWS, QK_DIM)
    per_row = lambda t: jnp.repeat(t.reshape(bsz, N_KV_HEADS, QK_DIM), N_MAPS, axis=1)
    small = pl.BlockSpec((1, HD), lambda b, c, pt: (0, 0))
    rows_spec = pl.BlockSpec((1, DEC_ROWS, QK_DIM), lambda b, c, pt: (b, 0, 0))
    n_pool = cache_k.shape[1]
    cache_k = cache_k.reshape(-1, page * N_KV_HEADS, QK_DIM)
    cache_v = cache_v.reshape(-1, page * N_KV_HEADS, V_DIM)
    page_specs = [pl.BlockSpec((None, page * N_KV_HEADS, QK_DIM),
                               functools.partial(lambda b, c, pt, j: (layer * n_pool + pt[b, c * npg + j], 0, 0), j=j))
                  for j in range(npg)]
    grid_spec = pltpu.PrefetchScalarGridSpec(
        num_scalar_prefetch=1,
        grid=(bsz, n_pages // npg),
        in_specs=[small, small, small, small,
                  pl.BlockSpec((1, V_DIM), lambda b, c, pt: (0, 0)),
                  rows_spec, rows_spec, rows_spec] + page_specs + page_specs,
        out_specs=pl.BlockSpec((1, N_HEADS, V_DIM), lambda b, c, pt: (b, 0, 0)),
        scratch_shapes=[pltpu.VMEM((DEC_ROWS, 1), F32), pltpu.VMEM((DEC_ROWS, 1), F32),
                        pltpu.VMEM((DEC_ROWS, V_DIM), F32)])
    o = pl.pallas_call(
        _decode_kernel,
        grid_spec=grid_spec,
        out_shape=jax.ShapeDtypeStruct((bsz, N_HEADS, V_DIM), F32),
        compiler_params=_cparams(2),
    )(page_table, *lams, subw, qa, per_row(k_new), per_row(v_new), *([cache_k] * npg), *([cache_v] * npg))
    return o.reshape(bsz, N_HEADS * V_DIM)


N_POW = 24


SSM_TILE_GROUPS = 16
PAIRS_PER_TILE = SSM_TILE_GROUPS // 2


def _ssm_prep_kernel(lr_ref, li_ref, ldt_ref, cre_ref, cim_ref, bre_ref, bim_ref,
                     toep_ref, wsr_ref, wsi_ref, wcr_ref, wci_ref, a1r_ref, a1i_ref, acr_ref, aci_ref,
                     bdr_ref, bdi_ref, cdr_ref, cdi_ref):
    P, C, TL = SSM_STATE, SSM_GROUP, SSM_CHUNK
    hi = lax.Precision.HIGHEST
    ell = lax.broadcasted_iota(jnp.int32, (N_POW, 1), 0).astype(F32)
    eye = (lax.broadcasted_iota(jnp.int32, (P, P), 0) == lax.broadcasted_iota(jnp.int32, (P, P), 1)).astype(F32)
    col = lax.broadcasted_iota(jnp.int32, (1, CHUNK_COLS), 1)
    tile_slot = 2 * (pl.program_id(0) % PAIRS_PER_TILE)
    st_slot = lax.broadcasted_iota(jnp.int32, (1, SSM_TILE_GROUPS * P), 1) // P
    ch_slot = col // C
    wcr_ref[...] = jnp.zeros(wcr_ref.shape, BF16)
    wci_ref[...] = jnp.zeros(wci_ref.shape, BF16)
    for j in range(2):
        lr, li = lr_ref[j], li_ref[j]
        dt = jnp.exp(ldt_ref[j])
        mag = jnp.exp(ell * (lr * dt))
        ph = ell * (li * dt)
        p_re, p_im = mag * jnp.cos(ph), mag * jnp.sin(ph)
        a_re, a_im = p_re[1:2], p_im[1:2]
        a1r_ref[j], a1i_ref[j] = a_re, a_im
        acr_ref[j], aci_ref[j] = p_re[TL:TL + 1], p_im[TL:TL + 1]
        den = lr * lr + li * li
        nr, ni = a_re - 1.0, a_im
        co_re = (nr * lr + ni * li) / den
        co_im = (ni * lr - nr * li) / den
        b_re, b_im = bre_ref[j], bim_ref[j]
        bb_re = co_re * b_re - co_im * b_im
        bb_im = co_re * b_im + co_im * b_re
        c_re, c_im = cre_ref[j], cim_ref[j]
        g_re = jnp.concatenate([c_re * p_re[l:l + 1] - c_im * p_im[l:l + 1] for l in range(TL + 1)], axis=0)
        g_im = jnp.concatenate([c_re * p_im[l:l + 1] + c_im * p_re[l:l + 1] for l in range(TL + 1)], axis=0)
        kt = (lax.dot_general(bb_re, g_re[:CHUNK_COLS], _NT, precision=hi, preferred_element_type=F32)
              - lax.dot_general(bb_im, g_im[:CHUNK_COLS], _NT, precision=hi, preferred_element_type=F32))
        for s in range(TL):
            rows = slice(s * C, (s + 1) * C)
            shifted = pltpu.roll(kt, s * C, 1) if s else kt
            toep_ref[j, rows, :] = jnp.where(col >= s * C, shifted, 0.0).astype(BF16)
            e = TL - 1 - s
            wsr_ref[j, rows, :] = (p_re[e:e + 1] * bb_re - p_im[e:e + 1] * bb_im).astype(BF16)
            wsi_ref[j, rows, :] = (p_re[e:e + 1] * bb_im + p_im[e:e + 1] * bb_re).astype(BF16)
        blk = (slice(j * P, (j + 1) * P), slice(j * CHUNK_COLS, (j + 1) * CHUNK_COLS))
        wcr_ref[(0,) + blk] = lax.dot_general(eye, g_re[C:], _NT, precision=hi,
                                              preferred_element_type=F32).astype(BF16)
        wci_ref[(0,) + blk] = (-lax.dot_general(eye, g_im[C:], _NT, precision=hi,
                                                preferred_element_type=F32)).astype(BF16)
        mine = st_slot == tile_slot + j
        brow = slice(j * C, (j + 1) * C)
        bdr_ref[0, brow, :] = jnp.where(mine, jnp.tile(bb_re, (1, SSM_TILE_GROUPS)), 0.0).astype(BF16)
        bdi_ref[0, brow, :] = jnp.where(mine, jnp.tile(bb_im, (1, SSM_TILE_GROUPS)), 0.0).astype(BF16)
        mine_c = ch_slot == tile_slot + j
        crow = slice(j * P, (j + 1) * P)
        ct_re = lax.dot_general(eye, jnp.tile(c_re, (SSM_TILE_GROUPS, 1)), _NT, precision=hi, preferred_element_type=F32)
        ct_im = lax.dot_general(eye, jnp.tile(c_im, (SSM_TILE_GROUPS, 1)), _NT, precision=hi, preferred_element_type=F32)
        cdr_ref[0, crow, :] = jnp.where(mine_c, ct_re, 0.0).astype(BF16)
        cdi_ref[0, crow, :] = jnp.where(mine_c, -ct_im, 0.0).astype(BF16)


def _ssm_prep(lam_re, lam_im, log_dt, b_re, b_im, c_re, c_im):
    G, P, C = SSM_GROUPS, SSM_STATE, SSM_GROUP
    n_tiles = G // SSM_TILE_GROUPS
    pair = lambda *s: pl.BlockSpec((2,) + s, lambda q: (q, 0, 0))
    tile = lambda *s: pl.BlockSpec((1,) + s, lambda q: (q // PAIRS_PER_TILE, q % PAIRS_PER_TILE, 0))
    wc_spec = pl.BlockSpec((1, PAIR_STATES, 2 * CHUNK_COLS), lambda q: (q, 0, 0))
    outs = dict(
        toep=(pair(CHUNK_COLS, CHUNK_COLS), (G, CHUNK_COLS, CHUNK_COLS), BF16),
        ws_re=(pair(CHUNK_COLS, P), (G, CHUNK_COLS, P), BF16), ws_im=(pair(CHUNK_COLS, P), (G, CHUNK_COLS, P), BF16),
        wc_re=(wc_spec, (G // 2, PAIR_STATES, 2 * CHUNK_COLS), BF16),
        wc_im=(wc_spec, (G // 2, PAIR_STATES, 2 * CHUNK_COLS), BF16),
        a1_re=(pair(1, P), (G, 1, P), F32), a1_im=(pair(1, P), (G, 1, P), F32),
        ac_re=(pair(1, P), (G, 1, P), F32), ac_im=(pair(1, P), (G, 1, P), F32),
        bd_re=(tile(2 * C, SSM_TILE_GROUPS * P), (n_tiles, SSM_TILE_GROUPS * C, SSM_TILE_GROUPS * P), BF16),
        bd_im=(tile(2 * C, SSM_TILE_GROUPS * P), (n_tiles, SSM_TILE_GROUPS * C, SSM_TILE_GROUPS * P), BF16),
        cd_re=(tile(2 * P, SSM_TILE_GROUPS * C), (n_tiles, SSM_TILE_GROUPS * P, SSM_TILE_GROUPS * C), BF16),
        cd_im=(tile(2 * P, SSM_TILE_GROUPS * C), (n_tiles, SSM_TILE_GROUPS * P, SSM_TILE_GROUPS * C), BF16))
    res = pl.pallas_call(
        _ssm_prep_kernel,
        grid=(G // 2,),
        in_specs=[pair(1, P), pair(1, P), pair(1, 1), pair(C, P), pair(C, P), pair(C, P), pair(C, P)],
        out_specs=[v[0] for v in outs.values()],
        out_shape=[jax.ShapeDtypeStruct(v[1], v[2]) for v in outs.values()],
        compiler_params=_cparams(1),
    )(lam_re.reshape(G, 1, P), lam_im.reshape(G, 1, P), log_dt.reshape(G, 1, 1),
      c_re, c_im, jnp.swapaxes(b_re, 1, 2), jnp.swapaxes(b_im, 1, 2))
    return dict(zip(outs, res))


def _ssm_state_kernel(uc_ref, wsr_ref, wsi_ref, sre_ref, sim_ref):
    def both(w_ref):
        return jnp.concatenate([jnp.dot(uc_ref[j], w_ref[j], preferred_element_type=F32) for j in range(2)], axis=-1)
    sre_ref[...] = both(wsr_ref)
    sim_ref[...] = both(wsi_ref)


def _ssm_scan_kernel(ar_ref, ai_ref, sre_ref, sim_ref, hpr_ref, hpi_ref, hfr_ref, hfi_ref):
    ar, ai = ar_ref[...], ai_ref[...]
    batch, n_chunks, width = sre_ref.shape

    def body(k, carry):
        nxt = []
        for b in range(batch):
            hr, hi = carry[2 * b], carry[2 * b + 1]
            row = (b, pl.ds(k, 1), slice(None))
            hpr_ref[row] = hr
            hpi_ref[row] = hi
            nxt += [ar * hr - ai * hi + sre_ref[row], ar * hi + ai * hr + sim_ref[row]]
        return tuple(nxt)

    zero = jnp.zeros((1, width), F32)
    fin = lax.fori_loop(0, n_chunks, body, (zero,) * (2 * batch))
    for b in range(batch):
        hfr_ref[b:b + 1, :] = fin[2 * b]
        hfi_ref[b:b + 1, :] = fin[2 * b + 1]


def _ssm_out_kernel(uc_ref, toep_ref, hpr_ref, hpi_ref, wcr_ref, wci_ref, y_ref):
    y = jnp.concatenate([jnp.dot(uc_ref[j], toep_ref[j], preferred_element_type=F32) for j in range(2)], axis=-1)
    y += jnp.dot(hpr_ref[...].astype(BF16), wcr_ref[0], preferred_element_type=F32)
    y += jnp.dot(hpi_ref[...].astype(BF16), wci_ref[0], preferred_element_type=F32)
    y_ref[0] = y.astype(BF16)


def _prompt_ssm(uc, prep, *, batch, seq):
    G, P = SSM_GROUPS, SSM_STATE
    nk = seq // SSM_CHUNK
    rows = nk * batch

    pair3 = lambda a, b: pl.BlockSpec((2, a, b), lambda q: (q, 0, 0))
    s_spec = pl.BlockSpec((rows, PAIR_STATES), lambda q: (0, q))
    s_re, s_im = pl.pallas_call(
        _ssm_state_kernel,
        grid=(G // 2,),
        in_specs=[pair3(rows, CHUNK_COLS), pair3(CHUNK_COLS, P), pair3(CHUNK_COLS, P)],
        out_specs=[s_spec, s_spec],
        out_shape=[jax.ShapeDtypeStruct((rows, G * P), F32)] * 2,
        compiler_params=_cparams(1),
    )(uc, prep['ws_re'], prep['ws_im'])

    wb = 512
    a_spec = pl.BlockSpec((1, wb), lambda j: (0, j))
    seq_spec = pl.BlockSpec((batch, nk, wb), lambda j: (0, 0, j))
    fin_spec = pl.BlockSpec((batch, wb), lambda j: (0, j))
    hp_re, hp_im, hf_re, hf_im = pl.pallas_call(
        _ssm_scan_kernel,
        grid=(G * P // wb,),
        in_specs=[a_spec, a_spec, seq_spec, seq_spec],
        out_specs=[seq_spec, seq_spec, fin_spec, fin_spec],
        out_shape=[jax.ShapeDtypeStruct((batch, nk, G * P), F32)] * 2 + [jax.ShapeDtypeStruct((batch, G * P), F32)] * 2,
        compiler_params=_cparams(1),
    )(prep['ac_re'].reshape(1, G * P), prep['ac_im'].reshape(1, G * P),
      s_re.reshape(batch, nk, G * P), s_im.reshape(batch, nk, G * P))

    yc = pl.pallas_call(
        _ssm_out_kernel,
        grid=(G // 2,),
        in_specs=[pair3(rows, CHUNK_COLS), pair3(CHUNK_COLS, CHUNK_COLS), s_spec, s_spec,
                  pl.BlockSpec((1, PAIR_STATES, 2 * CHUNK_COLS), lambda q: (q, 0, 0)),
                  pl.BlockSpec((1, PAIR_STATES, 2 * CHUNK_COLS), lambda q: (q, 0, 0))],
        out_specs=pl.BlockSpec((1, rows, 2 * CHUNK_COLS), lambda q: (q, 0, 0)),
        out_shape=jax.ShapeDtypeStruct((G // 2, rows, 2 * CHUNK_COLS), BF16),
        compiler_params=_cparams(1),
    )(uc, prep['toep'], hp_re.reshape(rows, G * P), hp_im.reshape(rows, G * P), prep['wc_re'], prep['wc_im'])
    return yc, hf_re, hf_im


def _ssm_step_kernel(u_ref, h0r_ref, h0i_ref, ar_ref, ai_ref, bdr_ref, bdi_ref, cdr_ref, cdi_ref,
                     hr_ref, hi_ref, y_ref):
    ch = SSM_TILE_GROUPS * SSM_GROUP
    st = SSM_TILE_GROUPS * SSM_STATE
    for t in range(SSM_GROUPS // SSM_TILE_GROUPS):
        cs, ss = slice(t * ch, (t + 1) * ch), slice(t * st, (t + 1) * st)
        ub = u_ref[:, cs].astype(BF16)
        ar, ai, h0r, h0i = ar_ref[:, ss], ai_ref[:, ss], h0r_ref[:, ss], h0i_ref[:, ss]
        hr = ar * h0r - ai * h0i + jnp.dot(ub, bdr_ref[t], preferred_element_type=F32)
        hi = ar * h0i + ai * h0r + jnp.dot(ub, bdi_ref[t], preferred_element_type=F32)
        hr_ref[:, ss] = hr
        hi_ref[:, ss] = hi
        y_ref[:, cs] = (jnp.dot(hr.astype(BF16), cdr_ref[t], preferred_element_type=F32)
                        + jnp.dot(hi.astype(BF16), cdi_ref[t], preferred_element_type=F32))


def _sample_ssm(u, h0_re, h0_im, prep):
    bsz = u.shape[0]
    n_st = SSM_STATES
    args = (u, h0_re.reshape(bsz, n_st), h0_im.reshape(bsz, n_st),
            prep['a1_re'].reshape(1, n_st), prep['a1_im'].reshape(1, n_st),
            prep['bd_re'], prep['bd_im'], prep['cd_re'], prep['cd_im'])
    return pl.pallas_call(
        _ssm_step_kernel,
        grid=(1,),
        in_specs=[_const_spec(a.shape) for a in args],
        out_specs=[pl.BlockSpec((bsz, n_st), lambda i: (0, 0))] * 2 + [pl.BlockSpec((bsz, D_MODEL), lambda i: (0, 0))],
        out_shape=[jax.ShapeDtypeStruct((bsz, n_st), F32)] * 2 + [jax.ShapeDtypeStruct((bsz, D_MODEL), F32)],
        compiler_params=_cparams(1),
    )(*args)


FF_CHUNK = D_FF // 4


def _merge_kernel(x_ref, o_ref, ga_ref, gs_ref, y_ref, u_ref, g1_ref, sc2_ref, sh2_ref, g2_ref,
                  d_ref, wglu_ref, bglu_ref, wo_ref, ln1g_ref, ln1b_ref,
                  wup_ref, bup_ref, wdn_ref, bdn_ref, ln2g_ref, ln2b_ref, out_ref, x1_s, h2_s, *maybe_y_s):
    i = pl.program_id(0)
    slot = i % 2

    @pl.when(i == 0)
    def _():
        x1_s[1] = jnp.zeros(x1_s.shape[1:], F32)
        h2_s[1] = jnp.zeros(h2_s.shape[1:], BF16)

    x1p = x1_s[1 - slot]
    h2p = h2_s[1 - slot]

    def mlp_chunk(c):
        cs = slice(c * FF_CHUNK, (c + 1) * FF_CHUNK)
        a = jnp.maximum(jnp.dot(h2p, wup_ref[:, cs], preferred_element_type=F32) + bup_ref[:, cs], 0.0)
        return jnp.dot((a * a).astype(BF16), wdn_ref[cs, :], preferred_element_type=F32)

    f = mlp_chunk(0)
    if maybe_y_s:
        y_s = maybe_y_s[0]
        _from_chunk_layout(y_ref, y_s)
        y = jnp.concatenate([y_s[jb] for jb in range(D_MODEL // LANES)], axis=-1)
    else:
        y = y_ref[...]
    yd = y + d_ref[...] * u_ref[...]
    g = jax.nn.gelu(yd, approximate=True)
    f += mlp_chunk(1)
    z = jnp.dot(g.astype(BF16), wglu_ref[...], preferred_element_type=F32) + bglu_ref[...]
    y_ssm = g * jax.nn.sigmoid(z)
    mixed = (jax.nn.sigmoid(ga_ref[...].astype(F32)) * o_ref[...].astype(F32)
             + jax.nn.sigmoid(gs_ref[...].astype(F32)) * y_ssm)
    f += mlp_chunk(2)
    t = jnp.dot(mixed.astype(BF16), wo_ref[...], preferred_element_type=F32)
    x1 = _layer_norm(ALPHA * x_ref[...] + (1.0 + g1_ref[0]) * t) * ln1g_ref[...] + ln1b_ref[...]
    x1_s[slot] = x1
    f += mlp_chunk(3) + bdn_ref[...]
    h2_s[slot] = (_layer_norm(x1) * (1.0 + sc2_ref[0]) + sh2_ref[0]).astype(BF16)
    out_ref[...] = _layer_norm(ALPHA * x1p + (1.0 + g2_ref[0]) * f) * ln2g_ref[...] + ln2b_ref[...]


def _merge(x, o, ga, gs, y, u, mods, params, *, tm, rows_per_mod):
    T = x.shape[0]
    n = T // tm
    mod_rows = mods[0].shape[1]
    cur = lambda i: jnp.minimum(i, n - 1)
    prev = lambda i: jnp.maximum(i - 1, 0)
    mod_cur = pl.BlockSpec((1, mod_rows, D_MODEL), lambda i: (cur(i) * tm // rows_per_mod, 0, 0))
    mod_prev = pl.BlockSpec((1, mod_rows, D_MODEL), lambda i: (prev(i) * tm // rows_per_mod, 0, 0))
    row = pl.BlockSpec((tm, D_MODEL), lambda i: (cur(i), 0))
    chunked = y.ndim == 3
    y_spec = (pl.BlockSpec((SSM_GROUPS // 2, tm // SSM_CHUNK, 2 * CHUNK_COLS), lambda i: (0, cur(i), 0))
              if chunked else row)
    scratch = [pltpu.VMEM((2, tm, D_MODEL), F32), pltpu.VMEM((2, tm, D_MODEL), BF16)]
    if chunked:
        scratch.append(pltpu.VMEM((D_MODEL // LANES, tm, LANES), F32))
    return pl.pallas_call(
        _merge_kernel,
        grid=(n + 1,),
        in_specs=[row] * 4 + [y_spec, row] + [mod_cur] * 3 + [mod_prev] + [_const_spec(p.shape) for p in params],
        out_specs=pl.BlockSpec((tm, D_MODEL), lambda i: (prev(i), 0)),
        out_shape=jax.ShapeDtypeStruct((T, D_MODEL), F32),
        scratch_shapes=scratch,
        compiler_params=_cparams(1),
    )(x, o, ga, gs, y, u, *mods, *params)


def _split_mods(ada):
    return [ada[..., i * D_MODEL:(i + 1) * D_MODEL] for i in range(6)]


def kernel(x_prompt, x_sample, cache_k, cache_v, state_ssm_re, state_ssm_im, page_table, c_prompt, c_sample, w_in, lambda_q1, lambda_k1, lambda_q2, lambda_k2, subln_w, ssm_lambda_re, ssm_lambda_im, ssm_log_dt, ssm_b_re, ssm_b_im, ssm_c_re, ssm_c_im, ssm_d, w_glu, b_glu, w_o, w_ada, b_ada, ln1_g, ln1_b, ln2_g, ln2_b, w_up, b_up, w_down, b_down):
    assert w_in.shape[0] == DEPTH and x_sample.shape[1] == 1
    batch, seq, _ = x_prompt.shape
    dec_b = x_sample.shape[0]
    page = cache_k.shape[2]
    past_len = page_table.shape[1] * page
    l = 0
    row2 = lambda a: a[l].reshape(1, -1)

    pad = (-batch) % 8
    c_all = jnp.concatenate([c_prompt, jnp.zeros((pad, D_MODEL), F32), c_sample], axis=0)
    ada = _ada(c_all, w_ada[l], row2(b_ada))
    mods_p = _split_mods(ada[:batch].reshape(batch, 1, 6 * D_MODEL))
    mods_s = _split_mods(ada[batch + pad:].reshape(1, dec_b, 6 * D_MODEL))

    w_in_b = w_in[l].astype(BF16)
    lams = (row2(lambda_q1), row2(lambda_k1), row2(lambda_q2), row2(lambda_k2))
    subw = row2(subln_w)
    merge_params = (row2(ssm_d), w_glu[l].astype(BF16), row2(b_glu), w_o[l].astype(BF16), row2(ln1_g), row2(ln1_b),
                    w_up[l].astype(BF16), row2(b_up), w_down[l].astype(BF16), row2(b_down), row2(ln2_g), row2(ln2_b))
    prep = _ssm_prep(ssm_lambda_re[l], ssm_lambda_im[l], ssm_log_dt[l], ssm_b_re[l], ssm_b_im[l],
                     ssm_c_re[l], ssm_c_im[l])

    tm = 512
    T = batch * seq
    xp = x_prompt.reshape(T, D_MODEL)
    q, k, v, kb, vb, u, ga, gs, uc = _proj(xp, mods_p[1], mods_p[0], _rope_tables(jnp.arange(seq)), w_in_b,
                                           tm=tm, rows_per_mod=seq, table_blocks=seq // tm, chunk_u=True)
    o = _prompt_attention(q, kb, vb, lams, subw, batch=batch, seq=seq, tq=256, tk=512)
    yc, hf_re, hf_im = _prompt_ssm(uc, prep, batch=batch, seq=seq)
    yp = _merge(xp, o, ga, gs, yc, u, (mods_p[2], mods_p[4], mods_p[3], mods_p[5]), merge_params,
                tm=256, rows_per_mod=seq)

    xs = x_sample.reshape(dec_b, D_MODEL)
    pos_s = jnp.full((dec_b,), past_len, jnp.int32)
    qs, ks, vs, _, _, us, gas, gss = _proj(xs, mods_s[1], mods_s[0], _rope_tables(pos_s), w_in_b,
                                           tm=dec_b, rows_per_mod=dec_b, table_blocks=1, chunk_u=False)
    os_ = _decode_attention(qs, ks, vs, cache_k, cache_v, page_table, lams, subw, l).astype(BF16)
    hs_re, hs_im, y_s = _sample_ssm(us, state_ssm_re[l], state_ssm_im[l], prep)
    ys = _merge(xs, os_, gas, gss, y_s, us, (mods_s[2], mods_s[4], mods_s[3], mods_s[5]), merge_params,
                tm=dec_b, rows_per_mod=dec_b)

    st = (SSM_GROUPS, SSM_STATE)
    return (yp.reshape(batch, seq, D_MODEL), ys.reshape(dec_b, 1, D_MODEL),
            k.reshape(1, batch, seq, N_KV_HEADS, QK_DIM), v.reshape(1, batch, seq, N_KV_HEADS, V_DIM),
            hf_re.reshape(1, batch, *st), hf_im.reshape(1, batch, *st),
            ks.reshape(1, dec_b, 1, N_KV_HEADS, QK_DIM), vs.reshape(1, dec_b, 1, N_KV_HEADS, V_DIM),
            hs_re.reshape(1, dec_b, *st), hs_im.reshape(1, dec_b, *st))
```

```python
import functools
import math

import jax
import jax.numpy as jnp
from jax import lax
from jax.experimental import pallas as pl
from jax.experimental.pallas import tpu as pltpu

F32 = jnp.float32
BF16 = jnp.bfloat16

D_MODEL = 1024
N_HEADS = 8
N_KV_HEADS = 4
GQA_GROUP = N_HEADS // N_KV_HEADS
HD = 64
QK_DIM = 2 * HD
V_DIM = 2 * HD
ROT_DIM = HD // 4
ROPE_THETA = 500000.0
SSM_GROUP = 16
SSM_GROUPS = D_MODEL // SSM_GROUP
SSM_STATE = 64
SSM_STATES = SSM_GROUPS * SSM_STATE
D_FF = 4 * D_MODEL
DEPTH = 1
ALPHA = (2.0 * DEPTH) ** 0.25
LN_EPS = 1e-5
LAM_INIT = 0.8 - 0.6 * math.exp(-0.3 * 0)
Q_COLS = N_HEADS * QK_DIM
K_COLS = N_KV_HEADS * QK_DIM
V_COLS = N_KV_HEADS * V_DIM
IN_COLS = Q_COLS + K_COLS + V_COLS + 3 * D_MODEL

LANES = 128
SSM_CHUNK = 16
CHUNK_COLS = SSM_CHUNK * SSM_GROUP
PAIR_STATES = 2 * SSM_STATE
VMEM_LIMIT = 48 * 1024 * 1024

_NT = (((1,), (1,)), ((), ()))
QK_SCALE = math.log2(math.e) / math.sqrt(HD)


def _cparams(n_axes):
    return pltpu.CompilerParams(dimension_semantics=("arbitrary",) * n_axes, vmem_limit_bytes=VMEM_LIMIT)


def _const_spec(shape):
    nd = len(shape)
    return pl.BlockSpec(shape, lambda *_: (0,) * nd, pipeline_mode=pl.Buffered(1))


def _layer_norm(x):
    mu = jnp.mean(x, axis=-1, keepdims=True)
    xc = x - mu
    var = jnp.mean(xc * xc, axis=-1, keepdims=True)
    return xc * lax.rsqrt(var + LN_EPS)


def _ada_kernel(c_ref, w_ref, b_ref, o_ref):
    c = c_ref[...]
    s = c * jax.nn.sigmoid(c)
    o_ref[...] = jnp.dot(s.astype(BF16), w_ref[...].astype(BF16), preferred_element_type=F32) + b_ref[...]


def _ada(c_all, w_ada, b_ada):
    rows = c_all.shape[0]
    tn = D_MODEL
    return pl.pallas_call(
        _ada_kernel,
        grid=(6 * D_MODEL // tn,),
        in_specs=[pl.BlockSpec((rows, D_MODEL), lambda j: (0, 0)),
                  pl.BlockSpec((D_MODEL, tn), lambda j: (0, j)),
                  pl.BlockSpec((1, tn), lambda j: (0, j))],
        out_specs=pl.BlockSpec((rows, tn), lambda j: (0, j)),
        out_shape=jax.ShapeDtypeStruct((rows, 6 * D_MODEL), F32),
        compiler_params=_cparams(1),
    )(c_all, w_ada, b_ada)


GROUPS_PER_TILE = LANES // SSM_GROUP
CHUNKS_PER_TILE = LANES // SSM_GROUP


def _lane_group():
    return lax.broadcasted_iota(jnp.int32, (1, LANES), 1) // SSM_GROUP


def _to_chunk_layout(u_s, uc_ref):
    n_chunks = u_s.shape[1] // SSM_CHUNK
    lane_grp = _lane_group()
    for jb in range(SSM_GROUPS // GROUPS_PER_TILE):
        for half in range(SSM_CHUNK // CHUNKS_PER_TILE):
            acc = [None] * GROUPS_PER_TILE
            for t8 in range(CHUNKS_PER_TILE):
                src = u_s[jb, pl.ds(half * CHUNKS_PER_TILE + t8, n_chunks, stride=SSM_CHUNK), :]
                for gl in range(GROUPS_PER_TILE):
                    shift = ((t8 - gl) * SSM_GROUP) % LANES
                    r = pltpu.roll(src, shift, 1) if shift else src
                    acc[gl] = r if t8 == 0 else jnp.where(lane_grp == t8, r, acc[gl])
            for gl in range(GROUPS_PER_TILE):
                uc_ref[jb * GROUPS_PER_TILE + gl, :, half * LANES:(half + 1) * LANES] = acc[gl].astype(BF16)


def _from_chunk_layout(yc_ref, y_s):
    n_chunks = yc_ref.shape[1]
    lane_grp = _lane_group()
    for jb in range(SSM_GROUPS // GROUPS_PER_TILE):
        acc = [None] * SSM_CHUNK
        for gl in range(GROUPS_PER_TILE):
            g = jb * GROUPS_PER_TILE + gl
            for half in range(SSM_CHUNK // CHUNKS_PER_TILE):
                c0 = (g % 2) * CHUNK_COLS + half * LANES
                src = yc_ref[g // 2, :, c0:c0 + LANES].astype(F32)
                for t8 in range(CHUNKS_PER_TILE):
                    tl = half * CHUNKS_PER_TILE + t8
                    shift = ((gl - t8) * SSM_GROUP) % LANES
                    r = pltpu.roll(src, shift, 1) if shift else src
                    acc[tl] = r if gl == 0 else jnp.where(lane_grp == gl, r, acc[tl])
        for tl in range(SSM_CHUNK):
            y_s[jb, pl.ds(tl, n_chunks, stride=SSM_CHUNK), :] = acc[tl]


def _proj_kernel(x_ref, sc_ref, sh_ref, cos_ref, sa_ref, sb_ref, w_ref,
                 q_ref, k_ref, v_ref, kb_ref, vb_ref, u_ref, ga_ref, gs_ref, *maybe_chunked):
    h = (_layer_norm(x_ref[...]) * (1.0 + sc_ref[0]) + sh_ref[0]).astype(BF16)
    cos_t, sin_a, sin_b = cos_ref[...], sa_ref[...], sb_ref[...]

    def rope(t):
        up = pltpu.roll(t, LANES - ROT_DIM // 2, 1)
        dn = pltpu.roll(t, ROT_DIM // 2, 1)
        return t * cos_t + up * sin_a + dn * sin_b

    def mm(c0, c1):
        return jnp.dot(h, w_ref[:, c0:c1], preferred_element_type=F32)

    q = mm(0, Q_COLS)
    for s in range(Q_COLS // LANES):
        sl = slice(s * LANES, (s + 1) * LANES)
        q_ref[:, sl] = (rope(q[:, sl]) * QK_SCALE).astype(BF16)
    tm = x_ref.shape[0]
    k = mm(Q_COLS, Q_COLS + K_COLS)
    c0 = Q_COLS + K_COLS
    v = mm(c0, c0 + V_COLS)
    vb_ref[...] = v.astype(BF16)
    for n in range(N_KV_HEADS):
        sl = slice(n * LANES, (n + 1) * LANES)
        kr = rope(k[:, sl])
        kb_ref[:, sl] = kr.astype(BF16)
        k_ref[pl.ds(n, tm, stride=N_KV_HEADS), :] = kr
        v_ref[pl.ds(n, tm, stride=N_KV_HEADS), :] = v[:, sl]
    c0 += V_COLS
    u = mm(c0, c0 + D_MODEL)
    u_ref[...] = u
    c0 += D_MODEL
    ga_ref[...] = mm(c0, c0 + D_MODEL).astype(BF16)
    c0 += D_MODEL
    gs_ref[...] = mm(c0, c0 + D_MODEL).astype(BF16)
    if maybe_chunked:
        uc_ref, u_s = maybe_chunked
        for jb in range(D_MODEL // LANES):
            u_s[jb] = u[:, jb * LANES:(jb + 1) * LANES]
        _to_chunk_layout(u_s, uc_ref)


def _proj(x, sc, sh, tables, w_in, *, tm, rows_per_mod, table_blocks, chunk_u):
    T = x.shape[0]
    mod_rows = sc.shape[1]
    mod_spec = pl.BlockSpec((1, mod_rows, D_MODEL), lambda i: (i * tm // rows_per_mod, 0, 0))
    tab_spec = pl.BlockSpec((tm, LANES), lambda i: (i % table_blocks, 0))
    row = lambda w: pl.BlockSpec((tm, w), lambda i: (i, 0))
    outs = [(Q_COLS, BF16), (K_COLS, F32), (V_COLS, F32), (K_COLS, BF16), (V_COLS, BF16),
            (D_MODEL, F32), (D_MODEL, BF16), (D_MODEL, BF16)]
    out_specs = [row(w) for w, _ in outs]
    out_shape = [jax.ShapeDtypeStruct((T, w), dt) for w, dt in outs]
    for idx in (1, 2):
        out_specs[idx] = pl.BlockSpec((tm * N_KV_HEADS, QK_DIM), lambda i: (i, 0))
        out_shape[idx] = jax.ShapeDtypeStruct((T * N_KV_HEADS, QK_DIM), F32)
    if chunk_u:
        out_specs.append(pl.BlockSpec((SSM_GROUPS, tm // SSM_CHUNK, CHUNK_COLS), lambda i: (0, i, 0)))
        out_shape.append(jax.ShapeDtypeStruct((SSM_GROUPS, T // SSM_CHUNK, CHUNK_COLS), BF16))
    return pl.pallas_call(
        _proj_kernel,
        grid=(T // tm,),
        in_specs=[row(D_MODEL), mod_spec, mod_spec, tab_spec, tab_spec, tab_spec,
                  _const_spec((D_MODEL, IN_COLS))],
        out_specs=out_specs,
        out_shape=out_shape,
        scratch_shapes=[pltpu.VMEM((D_MODEL // LANES, tm, LANES), F32)] if chunk_u else [],
        compiler_params=_cparams(1),
    )(x, sc, sh, *tables, w_in)


def _rope_tables(pos):
    inv = ROPE_THETA ** (-jnp.arange(0, ROT_DIM, 2, dtype=F32) / ROT_DIM)
    ang = pos.astype(F32)[:, None] * inv[None, :]
    cos, sin = jnp.cos(ang), jnp.sin(ang)
    n, half = pos.shape[0], ROT_DIM // 2
    ones = jnp.ones((n, HD - ROT_DIM), F32)
    zeros = lambda w: jnp.zeros((n, w), F32)
    cos_t = jnp.concatenate([cos, cos, ones], axis=-1)
    sin_a = jnp.concatenate([-sin, zeros(HD - half)], axis=-1)
    sin_b = jnp.concatenate([zeros(half), sin, zeros(HD - ROT_DIM)], axis=-1)
    return tuple(jnp.tile(t, (1, LANES // HD)) for t in (cos_t, sin_a, sin_b))


def _lambda(lq1, lk1, lq2, lk2):
    return (jnp.exp(jnp.sum(lq1[...] * lk1[...], axis=-1, keepdims=True))
            - jnp.exp(jnp.sum(lq2[...] * lk2[...], axis=-1, keepdims=True)) + LAM_INIT)


def _sub_norm(o, subw):
    o = o * lax.rsqrt(jnp.mean(o * o, axis=-1, keepdims=True) + LN_EPS)
    return o * subw * (1.0 - LAM_INIT)


N_MAPS = 2 * GQA_GROUP


def _attn_kernel(lq1, lk1, lq2, lk2, subw_ref, q_ref, k_ref, v_ref, o_ref,
                 q4_s, s_s, mpart_s, m_s, v1_s, acc_s, *, tq, tk):
    qi = pl.program_id(2)
    rows = N_MAPS * tq

    @pl.when(qi == 0)
    def _():
        v1_s[:, :V_DIM] = v_ref[...]
        v1_s[:, V_DIM:] = jnp.ones((v1_s.shape[0], V_DIM), BF16)

    comp0 = lax.broadcasted_iota(jnp.int32, (1, QK_DIM), 1) < HD
    for g in range(GQA_GROUP):
        qg = q_ref[:, g * QK_DIM:(g + 1) * QK_DIM].astype(F32)
        q4_s[(2 * g) * tq:(2 * g + 1) * tq, :] = jnp.where(comp0, qg, 0.0).astype(BF16)
        q4_s[(2 * g + 1) * tq:(2 * g + 2) * tq, :] = jnp.where(comp0, 0.0, qg).astype(BF16)
    n_blk = ((qi + 1) * tq - 1) // tk + 1
    mpart_s[...] = jnp.full(mpart_s.shape, -jnp.inf, F32)

    def scores(ki, masked):
        start = pl.multiple_of(ki * tk, tk)
        s = lax.dot_general(q4_s[...], k_ref[pl.ds(start, tk), :], _NT, preferred_element_type=F32)
        if masked:
            q_pos = qi * tq + lax.broadcasted_iota(jnp.int32, (tq, tk), 0)
            keep = q_pos >= start + lax.broadcasted_iota(jnp.int32, (tq, tk), 1)
            s = jnp.concatenate([jnp.where(keep, s[mp * tq:(mp + 1) * tq], -jnp.inf) for mp in range(N_MAPS)], axis=0)
        s_s[ki] = s
        m = mpart_s[...]
        for j in range(tk // LANES):
            m = jnp.maximum(m, s[:, j * LANES:(j + 1) * LANES])
        mpart_s[...] = m

    def pairs(fn, n):
        def four(i, carry):
            for r in range(4):
                fn(4 * i + r)
            return carry

        lax.fori_loop(0, n // 4, four, 0)

        @pl.when(n % 4 >= 2)
        def _():
            fn((n // 4) * 4)
            fn((n // 4) * 4 + 1)

        @pl.when(n % 2 == 1)
        def _():
            fn(n - 1)

    pairs(lambda ki: scores(ki, False), n_blk - 1)
    scores(n_blk - 1, True)

    m_s[...] = jnp.broadcast_to(jnp.max(mpart_s[...], axis=-1, keepdims=True), m_s.shape)
    acc_s[...] = jnp.zeros(acc_s.shape, F32)

    def weights(ki):
        start = pl.multiple_of(ki * tk, tk)
        s = s_s[ki]
        m = m_s[...]
        p = jnp.concatenate([jnp.exp2((s[:, j * LANES:(j + 1) * LANES] - m).astype(BF16))
                             for j in range(tk // LANES)], axis=-1)
        acc_s[...] += jnp.dot(p, v1_s[pl.ds(start, tk), :], preferred_element_type=F32)

    pairs(weights, n_blk)

    lam = _lambda(lq1, lk1, lq2, lk2)
    o = acc_s[:, :V_DIM] / acc_s[:, V_DIM:V_DIM + 1]
    for g in range(GQA_GROUP):
        d = o[(2 * g) * tq:(2 * g + 1) * tq] - lam * o[(2 * g + 1) * tq:(2 * g + 2) * tq]
        o_ref[:, g * V_DIM:(g + 1) * V_DIM] = _sub_norm(d, subw_ref[...]).astype(BF16)


def _prompt_attention(q, kb, vb, lams, subw, *, batch, seq, tq, tk):
    assert tk % tq == 0 and tq & (tq - 1) == 0
    T = batch * seq
    nq = seq // tq
    rows = N_MAPS * tq
    small = pl.BlockSpec((1, HD), lambda b, n, i: (0, 0))
    return pl.pallas_call(
        functools.partial(_attn_kernel, tq=tq, tk=tk),
        grid=(batch, N_KV_HEADS, nq),
        in_specs=[small, small, small, small,
                  pl.BlockSpec((1, V_DIM), lambda b, n, i: (0, 0)),
                  pl.BlockSpec((tq, GQA_GROUP * QK_DIM), lambda b, n, i: (b * nq + i, n)),
                  pl.BlockSpec((seq, QK_DIM), lambda b, n, i: (b, n)),
                  pl.BlockSpec((seq, V_DIM), lambda b, n, i: (b, n))],
        out_specs=pl.BlockSpec((tq, GQA_GROUP * V_DIM), lambda b, n, i: (b * nq + i, n)),
        out_shape=jax.ShapeDtypeStruct((T, N_HEADS * V_DIM), BF16),
        scratch_shapes=[pltpu.VMEM((rows, QK_DIM), BF16),
                        pltpu.VMEM((seq // tk, rows, tk), F32),
                        pltpu.VMEM((rows, LANES), F32),
                        pltpu.VMEM((rows, LANES), F32),
                        pltpu.VMEM((seq, 2 * V_DIM), BF16),
                        pltpu.VMEM((rows, 2 * V_DIM), F32)],
        compiler_params=_cparams(3),
    )(*lams, subw, q, kb, vb)


DEC_PAGES_PER_STEP = 32
DEC_ROWS = N_KV_HEADS * N_MAPS


def _decode_kernel(pt_ref, lq1, lk1, lq2, lk2, subw_ref, qa_ref, knew_ref, vnew_ref, *rest):
    del pt_ref
    npg = DEC_PAGES_PER_STEP
    k_refs, v_refs = rest[:npg], rest[npg:2 * npg]
    o_ref, m_s, l_s, acc_s = rest[2 * npg:]
    c = pl.program_id(1)
    qa = qa_ref[0]
    page_rows = k_refs[0].shape[0]

    @pl.when(c == 0)
    def _():
        m_s[...] = jnp.sum(qa.astype(F32) * knew_ref[0], axis=-1, keepdims=True)
        l_s[...] = jnp.ones(l_s.shape, F32)
        acc_s[...] = vnew_ref[0]

    own_head = (jnp.bitwise_and(lax.broadcasted_iota(jnp.int32, (DEC_ROWS, page_rows), 1), N_KV_HEADS - 1)
                == lax.broadcasted_iota(jnp.int32, (DEC_ROWS, page_rows), 0) // N_MAPS)
    s = jnp.concatenate(
        [jnp.where(own_head, lax.dot_general(qa, k_refs[j][...].astype(BF16), _NT, preferred_element_type=F32),
                   -jnp.inf) for j in range(npg)], axis=-1)
    m_prev = m_s[...]
    m_new = jnp.maximum(m_prev, jnp.max(s, axis=-1, keepdims=True))
    alpha = jnp.exp2(m_prev - m_new)
    p = jnp.exp2(s - m_new)
    l_s[...] = alpha * l_s[...] + jnp.sum(p, axis=-1, keepdims=True)
    pv = jnp.dot(p[:, :page_rows].astype(BF16), v_refs[0][...].astype(BF16), preferred_element_type=F32)
    for j in range(1, npg):
        pv += jnp.dot(p[:, j * page_rows:(j + 1) * page_rows].astype(BF16), v_refs[j][...].astype(BF16),
                      preferred_element_type=F32)
    acc_s[...] = alpha * acc_s[...] + pv
    m_s[...] = m_new

    @pl.when(c == pl.num_programs(1) - 1)
    def _():
        lam = _lambda(lq1, lk1, lq2, lk2)
        o = acc_s[...] / l_s[...]
        for n in range(N_KV_HEADS):
            r = n * N_MAPS
            d = o[r:r + GQA_GROUP] - lam * o[r + GQA_GROUP:r + 2 * GQA_GROUP]
            o_ref[0, n * GQA_GROUP:(n + 1) * GQA_GROUP, :] = _sub_norm(d, subw_ref[...])


def _decode_attention(q, k_new, v_new, cache_k, cache_v, page_table, lams, subw, layer):
    bsz = q.shape[0]
    page = cache_k.shape[2]
    n_pages = page_table.shape[1]
    npg = DEC_PAGES_PER_STEP
    q5 = q.reshape(bsz, N_KV_HEADS, GQA_GROUP, 2, HD)
    qa = jnp.einsum('bngcd,ce->bncged', q5, jnp.eye(2, dtype=q.dtype)).reshape(bsz, DEC_ROWS, QK_DIM)
    per_row = lambda t: jnp.repeat(t.reshape(bsz, N_KV_HEADS, QK_DIM), N_MAPS, axis=1)
    small = pl.BlockSpec((1, HD), lambda b, c, pt: (0, 0))
    rows_spec = pl.BlockSpec((1, DEC_ROWS, QK_DIM), lambda b, c, pt: (b, 0, 0))
    n_pool = cache_k.shape[1]
    cache_k = cache_k.reshape(-1, page * N_KV_HEADS, QK_DIM)
    cache_v = cache_v.reshape(-1, page * N_KV_HEADS, V_DIM)
    page_specs = [pl.BlockSpec((None, page * N_KV_HEADS, QK_DIM),
                               functools.partial(lambda b, c, pt, j: (layer * n_pool + pt[b, c * npg + j], 0, 0), j=j))
                  for j in range(npg)]
    grid_spec = pltpu.PrefetchScalarGridSpec(
        num_scalar_prefetch=1,
        grid=(bsz, n_pages // npg),
        in_specs=[small, small, small, small,
                  pl.BlockSpec((1, V_DIM), lambda b, c, pt: (0, 0)),
                  rows_spec, rows_spec, rows_spec] + page_specs + page_specs,
        out_specs=pl.BlockSpec((1, N_HEADS, V_DIM), lambda b, c, pt: (b, 0, 0)),
        scratch_shapes=[pltpu.VMEM((DEC_ROWS, 1), F32), pltpu.VMEM((DEC_ROWS, 1), F32),
                        pltpu.VMEM((DEC_ROWS, V_DIM), F32)])
    o = pl.pallas_call(
        _decode_kernel,
        grid_spec=grid_spec,
        out_shape=jax.ShapeDtypeStruct((bsz, N_HEADS, V_DIM), F32),
        compiler_params=_cparams(2),
    )(page_table, *lams, subw, qa, per_row(k_new), per_row(v_new), *([cache_k] * npg), *([cache_v] * npg))
    return o.reshape(bsz, N_HEADS * V_DIM)


N_POW = 24


SSM_TILE_GROUPS = 16
PAIRS_PER_TILE = SSM_TILE_GROUPS // 2


def _ssm_prep_kernel(lr_ref, li_ref, ldt_ref, cre_ref, cim_ref, bre_ref, bim_ref,
                     toep_ref, wsr_ref, wsi_ref, wcr_ref, wci_ref, a1r_ref, a1i_ref, acr_ref, aci_ref,
                     bdr_ref, bdi_ref, cdr_ref, cdi_ref):
    P, C, TL = SSM_STATE, SSM_GROUP, SSM_CHUNK
    hi = lax.Precision.HIGHEST
    ell = lax.broadcasted_iota(jnp.int32, (N_POW, 1), 0).astype(F32)
    eye = (lax.broadcasted_iota(jnp.int32, (P, P), 0) == lax.broadcasted_iota(jnp.int32, (P, P), 1)).astype(F32)
    col = lax.broadcasted_iota(jnp.int32, (1, CHUNK_COLS), 1)
    tile_slot = 2 * (pl.program_id(0) % PAIRS_PER_TILE)
    st_slot = lax.broadcasted_iota(jnp.int32, (1, SSM_TILE_GROUPS * P), 1) // P
    ch_slot = col // C
    wcr_ref[...] = jnp.zeros(wcr_ref.shape, BF16)
    wci_ref[...] = jnp.zeros(wci_ref.shape, BF16)
    for j in range(2):
        lr, li = lr_ref[j], li_ref[j]
        dt = jnp.exp(ldt_ref[j])
        mag = jnp.exp(ell * (lr * dt))
        ph = ell * (li * dt)
        p_re, p_im = mag * jnp.cos(ph), mag * jnp.sin(ph)
        a_re, a_im = p_re[1:2], p_im[1:2]
        a1r_ref[j], a1i_ref[j] = a_re, a_im
        acr_ref[j], aci_ref[j] = p_re[TL:TL + 1], p_im[TL:TL + 1]
        den = lr * lr + li * li
        nr, ni = a_re - 1.0, a_im
        co_re = (nr * lr + ni * li) / den
        co_im = (ni * lr - nr * li) / den
        b_re, b_im = bre_ref[j], bim_ref[j]
        bb_re = co_re * b_re - co_im * b_im
        bb_im = co_re * b_im + co_im * b_re
        c_re, c_im = cre_ref[j], cim_ref[j]
        g_re = jnp.concatenate([c_re * p_re[l:l + 1] - c_im * p_im[l:l + 1] for l in range(TL + 1)], axis=0)
        g_im = jnp.concatenate([c_re * p_im[l:l + 1] + c_im * p_re[l:l + 1] for l in range(TL + 1)], axis=0)
        kt = (lax.dot_general(bb_re, g_re[:CHUNK_COLS], _NT, precision=hi, preferred_element_type=F32)
              - lax.dot_general(bb_im, g_im[:CHUNK_COLS], _NT, precision=hi, preferred_element_type=F32))
        for s in range(TL):
            rows = slice(s * C, (s + 1) * C)
            shifted = pltpu.roll(kt, s * C, 1) if s else kt
            toep_ref[j, rows, :] = jnp.where(col >= s * C, shifted, 0.0).astype(BF16)
            e = TL - 1 - s
            wsr_ref[j, rows, :] = (p_re[e:e + 1] * bb_re - p_im[e:e + 1] * bb_im).astype(BF16)
            wsi_ref[j, rows, :] = (p_re[e:e + 1] * bb_im + p_im[e:e + 1] * bb_re).astype(BF16)
        blk = (slice(j * P, (j + 1) * P), slice(j * CHUNK_COLS, (j + 1) * CHUNK_COLS))
        wcr_ref[(0,) + blk] = lax.dot_general(eye, g_re[C:], _NT, precision=hi,
                                              preferred_element_type=F32).astype(BF16)
        wci_ref[(0,) + blk] = (-lax.dot_general(eye, g_im[C:], _NT, precision=hi,
                                                preferred_element_type=F32)).astype(BF16)
        mine = st_slot == tile_slot + j
        brow = slice(j * C, (j + 1) * C)
        bdr_ref[0, brow, :] = jnp.where(mine, jnp.tile(bb_re, (1, SSM_TILE_GROUPS)), 0.0).astype(BF16)
        bdi_ref[0, brow, :] = jnp.where(mine, jnp.tile(bb_im, (1, SSM_TILE_GROUPS)), 0.0).astype(BF16)
        mine_c = ch_slot == tile_slot + j
        crow = slice(j * P, (j + 1) * P)
        ct_re = lax.dot_general(eye, jnp.tile(c_re, (SSM_TILE_GROUPS, 1)), _NT, precision=hi, preferred_element_type=F32)
        ct_im = lax.dot_general(eye, jnp.tile(c_im, (SSM_TILE_GROUPS, 1)), _NT, precision=hi, preferred_element_type=F32)
        cdr_ref[0, crow, :] = jnp.where(mine_c, ct_re, 0.0).astype(BF16)
        cdi_ref[0, crow, :] = jnp.where(mine_c, -ct_im, 0.0).astype(BF16)


def _ssm_prep(lam_re, lam_im, log_dt, b_re, b_im, c_re, c_im):
    G, P, C = SSM_GROUPS, SSM_STATE, SSM_GROUP
    n_tiles = G // SSM_TILE_GROUPS
    pair = lambda *s: pl.BlockSpec((2,) + s, lambda q: (q, 0, 0))
    tile = lambda *s: pl.BlockSpec((1,) + s, lambda q: (q // PAIRS_PER_TILE, q % PAIRS_PER_TILE, 0))
    wc_spec = pl.BlockSpec((1, PAIR_STATES, 2 * CHUNK_COLS), lambda q: (q, 0, 0))
    outs = dict(
        toep=(pair(CHUNK_COLS, CHUNK_COLS), (G, CHUNK_COLS, CHUNK_COLS), BF16),
        ws_re=(pair(CHUNK_COLS, P), (G, CHUNK_COLS, P), BF16), ws_im=(pair(CHUNK_COLS, P), (G, CHUNK_COLS, P), BF16),
        wc_re=(wc_spec, (G // 2, PAIR_STATES, 2 * CHUNK_COLS), BF16),
        wc_im=(wc_spec, (G // 2, PAIR_STATES, 2 * CHUNK_COLS), BF16),
        a1_re=(pair(1, P), (G, 1, P), F32), a1_im=(pair(1, P), (G, 1, P), F32),
        ac_re=(pair(1, P), (G, 1, P), F32), ac_im=(pair(1, P), (G, 1, P), F32),
        bd_re=(tile(2 * C, SSM_TILE_GROUPS * P), (n_tiles, SSM_TILE_GROUPS * C, SSM_TILE_GROUPS * P), BF16),
        bd_im=(tile(2 * C, SSM_TILE_GROUPS * P), (n_tiles, SSM_TILE_GROUPS * C, SSM_TILE_GROUPS * P), BF16),
        cd_re=(tile(2 * P, SSM_TILE_GROUPS * C), (n_tiles, SSM_TILE_GROUPS * P, SSM_TILE_GROUPS * C), BF16),
        cd_im=(tile(2 * P, SSM_TILE_GROUPS * C), (n_tiles, SSM_TILE_GROUPS * P, SSM_TILE_GROUPS * C), BF16))
    res = pl.pallas_call(
        _ssm_prep_kernel,
        grid=(G // 2,),
        in_specs=[pair(1, P), pair(1, P), pair(1, 1), pair(C, P), pair(C, P), pair(C, P), pair(C, P)],
        out_specs=[v[0] for v in outs.values()],
        out_shape=[jax.ShapeDtypeStruct(v[1], v[2]) for v in outs.values()],
        compiler_params=_cparams(1),
    )(lam_re.reshape(G, 1, P), lam_im.reshape(G, 1, P), log_dt.reshape(G, 1, 1),
      c_re, c_im, jnp.swapaxes(b_re, 1, 2), jnp.swapaxes(b_im, 1, 2))
    return dict(zip(outs, res))


def _ssm_state_kernel(uc_ref, wsr_ref, wsi_ref, sre_ref, sim_ref):
    def both(w_ref):
        return jnp.concatenate([jnp.dot(uc_ref[j], w_ref[j], preferred_element_type=F32) for j in range(2)], axis=-1)
    sre_ref[...] = both(wsr_ref)
    sim_ref[...] = both(wsi_ref)


def _ssm_scan_kernel(ar_ref, ai_ref, sre_ref, sim_ref, hpr_ref, hpi_ref, hfr_ref, hfi_ref):
    ar, ai = ar_ref[...], ai_ref[...]
    batch, n_chunks, width = sre_ref.shape

    def body(k, carry):
        nxt = []
        for b in range(batch):
            hr, hi = carry[2 * b], carry[2 * b + 1]
            row = (b, pl.ds(k, 1), slice(None))
            hpr_ref[row] = hr
            hpi_ref[row] = hi
            nxt += [ar * hr - ai * hi + sre_ref[row], ar * hi + ai * hr + sim_ref[row]]
        return tuple(nxt)

    zero = jnp.zeros((1, width), F32)
    fin = lax.fori_loop(0, n_chunks, body, (zero,) * (2 * batch))
    for b in range(batch):
        hfr_ref[b:b + 1, :] = fin[2 * b]
        hfi_ref[b:b + 1, :] = fin[2 * b + 1]


def _ssm_out_kernel(uc_ref, toep_ref, hpr_ref, hpi_ref, wcr_ref, wci_ref, y_ref):
    y = jnp.concatenate([jnp.dot(uc_ref[j], toep_ref[j], preferred_element_type=F32) for j in range(2)], axis=-1)
    y += jnp.dot(hpr_ref[...].astype(BF16), wcr_ref[0], preferred_element_type=F32)
    y += jnp.dot(hpi_ref[...].astype(BF16), wci_ref[0], preferred_element_type=F32)
    y_ref[0] = y.astype(BF16)


def _prompt_ssm(uc, prep, *, batch, seq):
    G, P = SSM_GROUPS, SSM_STATE
    nk = seq // SSM_CHUNK
    rows = nk * batch

    pair3 = lambda a, b: pl.BlockSpec((2, a, b), lambda q: (q, 0, 0))
    s_spec = pl.BlockSpec((rows, PAIR_STATES), lambda q: (0, q))
    s_re, s_im = pl.pallas_call(
        _ssm_state_kernel,
        grid=(G // 2,),
        in_specs=[pair3(rows, CHUNK_COLS), pair3(CHUNK_COLS, P), pair3(CHUNK_COLS, P)],
        out_specs=[s_spec, s_spec],
        out_shape=[jax.ShapeDtypeStruct((rows, G * P), F32)] * 2,
        compiler_params=_cparams(1),
    )(uc, prep['ws_re'], prep['ws_im'])

    wb = 512
    a_spec = pl.BlockSpec((1, wb), lambda j: (0, j))
    seq_spec = pl.BlockSpec((batch, nk, wb), lambda j: (0, 0, j))
    fin_spec = pl.BlockSpec((batch, wb), lambda j: (0, j))
    hp_re, hp_im, hf_re, hf_im = pl.pallas_call(
        _ssm_scan_kernel,
        grid=(G * P // wb,),
        in_specs=[a_spec, a_spec, seq_spec, seq_spec],
        out_specs=[seq_spec, seq_spec, fin_spec, fin_spec],
        out_shape=[jax.ShapeDtypeStruct((batch, nk, G * P), F32)] * 2 + [jax.ShapeDtypeStruct((batch, G * P), F32)] * 2,
        compiler_params=_cparams(1),
    )(prep['ac_re'].reshape(1, G * P), prep['ac_im'].reshape(1, G * P),
      s_re.reshape(batch, nk, G * P), s_im.reshape(batch, nk, G * P))

    yc = pl.pallas_call(
        _ssm_out_kernel,
        grid=(G // 2,),
        in_specs=[pair3(rows, CHUNK_COLS), pair3(CHUNK_COLS, CHUNK_COLS), s_spec, s_spec,
                  pl.BlockSpec((1, PAIR_STATES, 2 * CHUNK_COLS), lambda q: (q, 0, 0)),
                  pl.BlockSpec((1, PAIR_STATES, 2 * CHUNK_COLS), lambda q: (q, 0, 0))],
        out_specs=pl.BlockSpec((1, rows, 2 * CHUNK_COLS), lambda q: (q, 0, 0)),
        out_shape=jax.ShapeDtypeStruct((G // 2, rows, 2 * CHUNK_COLS), BF16),
        compiler_params=_cparams(1),
    )(uc, prep['toep'], hp_re.reshape(rows, G * P), hp_im.reshape(rows, G * P), prep['wc_re'], prep['wc_im'])
    return yc, hf_re, hf_im


def _ssm_step_kernel(u_ref, h0r_ref, h0i_ref, ar_ref, ai_ref, bdr_ref, bdi_ref, cdr_ref, cdi_ref,
                     hr_ref, hi_ref, y_ref):
    ch = SSM_TILE_GROUPS * SSM_GROUP
    st = SSM_TILE_GROUPS * SSM_STATE
    for t in range(SSM_GROUPS // SSM_TILE_GROUPS):
        cs, ss = slice(t * ch, (t + 1) * ch), slice(t * st, (t + 1) * st)
        ub = u_ref[:, cs].astype(BF16)
        ar, ai, h0r, h0i = ar_ref[:, ss], ai_ref[:, ss], h0r_ref[:, ss], h0i_ref[:, ss]
        hr = ar * h0r - ai * h0i + jnp.dot(ub, bdr_ref[t], preferred_element_type=F32)
        hi = ar * h0i + ai * h0r + jnp.dot(ub, bdi_ref[t], preferred_element_type=F32)
        hr_ref[:, ss] = hr
        hi_ref[:, ss] = hi
        y_ref[:, cs] = (jnp.dot(hr.astype(BF16), cdr_ref[t], preferred_element_type=F32)
                        + jnp.dot(hi.astype(BF16), cdi_ref[t], preferred_element_type=F32))


def _sample_ssm(u, h0_re, h0_im, prep):
    bsz = u.shape[0]
    n_st = SSM_STATES
    args = (u, h0_re.reshape(bsz, n_st), h0_im.reshape(bsz, n_st),
            prep['a1_re'].reshape(1, n_st), prep['a1_im'].reshape(1, n_st),
            prep['bd_re'], prep['bd_im'], prep['cd_re'], prep['cd_im'])
    return pl.pallas_call(
        _ssm_step_kernel,
        grid=(1,),
        in_specs=[_const_spec(a.shape) for a in args],
        out_specs=[pl.BlockSpec((bsz, n_st), lambda i: (0, 0))] * 2 + [pl.BlockSpec((bsz, D_MODEL), lambda i: (0, 0))],
        out_shape=[jax.ShapeDtypeStruct((bsz, n_st), F32)] * 2 + [jax.ShapeDtypeStruct((bsz, D_MODEL), F32)],
        compiler_params=_cparams(1),
    )(*args)


FF_CHUNK = D_FF // 4


def _merge_kernel(x_ref, o_ref, ga_ref, gs_ref, y_ref, u_ref, g1_ref, sc2_ref, sh2_ref, g2_ref,
                  d_ref, wglu_ref, bglu_ref, wo_ref, ln1g_ref, ln1b_ref,
                  wup_ref, bup_ref, wdn_ref, bdn_ref, ln2g_ref, ln2b_ref, out_ref, x1_s, h2_s, *maybe_y_s):
    i = pl.program_id(0)
    slot = i % 2

    @pl.when(i == 0)
    def _():
        x1_s[1] = jnp.zeros(x1_s.shape[1:], F32)
        h2_s[1] = jnp.zeros(h2_s.shape[1:], BF16)

    x1p = x1_s[1 - slot]
    h2p = h2_s[1 - slot]

    def mlp_chunk(c):
        cs = slice(c * FF_CHUNK, (c + 1) * FF_CHUNK)
        a = jnp.maximum(jnp.dot(h2p, wup_ref[:, cs], preferred_element_type=F32) + bup_ref[:, cs], 0.0)
        return jnp.dot((a * a).astype(BF16), wdn_ref[cs, :], preferred_element_type=F32)

    f = mlp_chunk(0)
    if maybe_y_s:
        y_s = maybe_y_s[0]
        _from_chunk_layout(y_ref, y_s)
        y = jnp.concatenate([y_s[jb] for jb in range(D_MODEL // LANES)], axis=-1)
    else:
        y = y_ref[...]
    yd = y + d_ref[...] * u_ref[...]
    g = jax.nn.gelu(yd, approximate=True)
    f += mlp_chunk(1)
    z = jnp.dot(g.astype(BF16), wglu_ref[...], preferred_element_type=F32) + bglu_ref[...]
    y_ssm = g * jax.nn.sigmoid(z)
    mixed = (jax.nn.sigmoid(ga_ref[...].astype(F32)) * o_ref[...].astype(F32)
             + jax.nn.sigmoid(gs_ref[...].astype(F32)) * y_ssm)
    f += mlp_chunk(2)
    t = jnp.dot(mixed.astype(BF16), wo_ref[...], preferred_element_type=F32)
    x1 = _layer_norm(ALPHA * x_ref[...] + (1.0 + g1_ref[0]) * t) * ln1g_ref[...] + ln1b_ref[...]
    x1_s[slot] = x1
    f += mlp_chunk(3) + bdn_ref[...]
    h2_s[slot] = (_layer_norm(x1) * (1.0 + sc2_ref[0]) + sh2_ref[0]).astype(BF16)
    out_ref[...] = _layer_norm(ALPHA * x1p + (1.0 + g2_ref[0]) * f) * ln2g_ref[...] + ln2b_ref[...]


def _merge(x, o, ga, gs, y, u, mods, params, *, tm, rows_per_mod):
    T = x.shape[0]
    n = T // tm
    mod_rows = mods[0].shape[1]
    cur = lambda i: jnp.minimum(i, n - 1)
    prev = lambda i: jnp.maximum(i - 1, 0)
    mod_cur = pl.BlockSpec((1, mod_rows, D_MODEL), lambda i: (cur(i) * tm // rows_per_mod, 0, 0))
    mod_prev = pl.BlockSpec((1, mod_rows, D_MODEL), lambda i: (prev(i) * tm // rows_per_mod, 0, 0))
    row = pl.BlockSpec((tm, D_MODEL), lambda i: (cur(i), 0))
    chunked = y.ndim == 3
    y_spec = (pl.BlockSpec((SSM_GROUPS // 2, tm // SSM_CHUNK, 2 * CHUNK_COLS), lambda i: (0, cur(i), 0))
              if chunked else row)
    scratch = [pltpu.VMEM((2, tm, D_MODEL), F32), pltpu.VMEM((2, tm, D_MODEL), BF16)]
    if chunked:
        scratch.append(pltpu.VMEM((D_MODEL // LANES, tm, LANES), F32))
    return pl.pallas_call(
        _merge_kernel,
        grid=(n + 1,),
        in_specs=[row] * 4 + [y_spec, row] + [mod_cur] * 3 + [mod_prev] + [_const_spec(p.shape) for p in params],
        out_specs=pl.BlockSpec((tm, D_MODEL), lambda i: (prev(i), 0)),
        out_shape=jax.ShapeDtypeStruct((T, D_MODEL), F32),
        scratch_shapes=scratch,
        compiler_params=_cparams(1),
    )(x, o, ga, gs, y, u, *mods, *params)


def _split_mods(ada):
    return [ada[..., i * D_MODEL:(i + 1) * D_MODEL] for i in range(6)]


def kernel(x_prompt, x_sample, cache_k, cache_v, state_ssm_re, state_ssm_im, page_table, c_prompt, c_sample, w_in, lambda_q1, lambda_k1, lambda_q2, lambda_k2, subln_w, ssm_lambda_re, ssm_lambda_im, ssm_log_dt, ssm_b_re, ssm_b_im, ssm_c_re, ssm_c_im, ssm_d, w_glu, b_glu, w_o, w_ada, b_ada, ln1_g, ln1_b, ln2_g, ln2_b, w_up, b_up, w_down, b_down):
    assert w_in.shape[0] == DEPTH and x_sample.shape[1] == 1
    batch, seq, _ = x_prompt.shape
    dec_b = x_sample.shape[0]
    page = cache_k.shape[2]
    past_len = page_table.shape[1] * page
    l = 0
    row2 = lambda a: a[l].reshape(1, -1)

    pad = (-batch) % 8
    c_all = jnp.concatenate([c_prompt, jnp.zeros((pad, D_MODEL), F32), c_sample], axis=0)
    ada = _ada(c_all, w_ada[l], row2(b_ada))
    mods_p = _split_mods(ada[:batch].reshape(batch, 1, 6 * D_MODEL))
    mods_s = _split_mods(ada[batch + pad:].reshape(1, dec_b, 6 * D_MODEL))

    w_in_b = w_in[l].astype(BF16)
    lams = (row2(lambda_q1), row2(lambda_k1), row2(lambda_q2), row2(lambda_k2))
    subw = row2(subln_w)
    merge_params = (row2(ssm_d), w_glu[l].astype(BF16), row2(b_glu), w_o[l].astype(BF16), row2(ln1_g), row2(ln1_b),
                    w_up[l].astype(BF16), row2(b_up), w_down[l].astype(BF16), row2(b_down), row2(ln2_g), row2(ln2_b))
    prep = _ssm_prep(ssm_lambda_re[l], ssm_lambda_im[l], ssm_log_dt[l], ssm_b_re[l], ssm_b_im[l],
                     ssm_c_re[l], ssm_c_im[l])

    tm = 512
    T = batch * seq
    xp = x_prompt.reshape(T, D_MODEL)
    q, k, v, kb, vb, u, ga, gs, uc = _proj(xp, mods_p[1], mods_p[0], _rope_tables(jnp.arange(seq)), w_in_b,
                                           tm=tm, rows_per_mod=seq, table_blocks=seq // tm, chunk_u=True)
    o = _prompt_attention(q, kb, vb, lams, subw, batch=batch, seq=seq, tq=256, tk=512)
    yc, hf_re, hf_im = _prompt_ssm(uc, prep, batch=batch, seq=seq)
    yp = _merge(xp, o, ga, gs, yc, u, (mods_p[2], mods_p[4], mods_p[3], mods_p[5]), merge_params,
                tm=256, rows_per_mod=seq)

    xs = x_sample.reshape(dec_b, D_MODEL)
    pos_s = jnp.full((dec_b,), past_len, jnp.int32)
    qs, ks, vs, _, _, us, gas, gss = _proj(xs, mods_s[1], mods_s[0], _rope_tables(pos_s), w_in_b,
                                           tm=dec_b, rows_per_mod=dec_b, table_blocks=1, chunk_u=False)
    os_ = _decode_attention(qs, ks, vs, cache_k, cache_v, page_table, lams, subw, l).astype(BF16)
    hs_re, hs_im, y_s = _sample_ssm(us, state_ssm_re[l], state_ssm_im[l], prep)
    ys = _merge(xs, os_, gas, gss, y_s, us, (mods_s[2], mods_s[4], mods_s[3], mods_s[5]), merge_params,
                tm=dec_b, rows_per_mod=dec_b)

    st = (SSM_GROUPS, SSM_STATE)
    return (yp.reshape(batch, seq, D_MODEL), ys.reshape(dec_b, 1, D_MODEL),
            k.reshape(1, batch, seq, N_KV_HEADS, QK_DIM), v.reshape(1, batch, seq, N_KV_HEADS, V_DIM),
            hf_re.reshape(1, batch, *st), hf_im.reshape(1, batch, *st),
            ks.reshape(1, dec_b, 1, N_KV_HEADS, QK_DIM), vs.reshape(1, dec_b, 1, N_KV_HEADS, V_DIM),
            hs_re.reshape(1, dec_b, *st), hs_im.reshape(1, dec_b, *st))
```

```python
import functools
import math

import jax
import jax.numpy as jnp
from jax import lax
from jax.experimental import pallas as pl
from jax.experimental.pallas import tpu as pltpu

F32 = jnp.float32
BF16 = jnp.bfloat16

D_MODEL = 1024
N_HEADS = 8
N_KV_HEADS = 4
GQA_GROUP = N_HEADS // N_KV_HEADS
HD = 64
QK_DIM = 2 * HD
V_DIM = 2 * HD
ROT_DIM = HD // 4
ROPE_THETA = 500000.0
SSM_GROUP = 16
SSM_GROUPS = D_MODEL // SSM_GROUP
SSM_STATE = 64
SSM_STATES = SSM_GROUPS * SSM_STATE
D_FF = 4 * D_MODEL
DEPTH = 1
ALPHA = (2.0 * DEPTH) ** 0.25
LN_EPS = 1e-5
LAM_INIT = 0.8 - 0.6 * math.exp(-0.3 * 0)
Q_COLS = N_HEADS * QK_DIM
K_COLS = N_KV_HEADS * QK_DIM
V_COLS = N_KV_HEADS * V_DIM
IN_COLS = Q_COLS + K_COLS + V_COLS + 3 * D_MODEL

LANES = 128
SSM_CHUNK = 16
CHUNK_COLS = SSM_CHUNK * SSM_GROUP
PAIR_STATES = 2 * SSM_STATE
VMEM_LIMIT = 48 * 1024 * 1024

_NT = (((1,), (1,)), ((), ()))
QK_SCALE = math.log2(math.e) / math.sqrt(HD)


def _cparams(n_axes):
    return pltpu.CompilerParams(dimension_semantics=("arbitrary",) * n_axes, vmem_limit_bytes=VMEM_LIMIT)


def _const_spec(shape):
    nd = len(shape)
    return pl.BlockSpec(shape, lambda *_: (0,) * nd, pipeline_mode=pl.Buffered(1))


def _layer_norm(x):
    mu = jnp.mean(x, axis=-1, keepdims=True)
    xc = x - mu
    var = jnp.mean(xc * xc, axis=-1, keepdims=True)
    return xc * lax.rsqrt(var + LN_EPS)


def _ada_kernel(c_ref, w_ref, b_ref, o_ref):
    c = c_ref[...]
    s = c * jax.nn.sigmoid(c)
    o_ref[...] = jnp.dot(s.astype(BF16), w_ref[...].astype(BF16), preferred_element_type=F32) + b_ref[...]


def _ada(c_all, w_ada, b_ada):
    rows = c_all.shape[0]
    tn = D_MODEL
    return pl.pallas_call(
        _ada_kernel,
        grid=(6 * D_MODEL // tn,),
        in_specs=[pl.BlockSpec((rows, D_MODEL), lambda j: (0, 0)),
                  pl.BlockSpec((D_MODEL, tn), lambda j: (0, j)),
                  pl.BlockSpec((1, tn), lambda j: (0, j))],
        out_specs=pl.BlockSpec((rows, tn), lambda j: (0, j)),
        out_shape=jax.ShapeDtypeStruct((rows, 6 * D_MODEL), F32),
        compiler_params=_cparams(1),
    )(c_all, w_ada, b_ada)


GROUPS_PER_TILE = LANES // SSM_GROUP
CHUNKS_PER_TILE = LANES // SSM_GROUP


def _lane_group():
    return lax.broadcasted_iota(jnp.int32, (1, LANES), 1) // SSM_GROUP


def _to_chunk_layout(u_s, uc_ref):
    n_chunks = u_s.shape[1] // SSM_CHUNK
    lane_grp = _lane_group()
    for jb in range(SSM_GROUPS // GROUPS_PER_TILE):
        for half in range(SSM_CHUNK // CHUNKS_PER_TILE):
            acc = [None] * GROUPS_PER_TILE
            for t8 in range(CHUNKS_PER_TILE):
                src = u_s[jb, pl.ds(half * CHUNKS_PER_TILE + t8, n_chunks, stride=SSM_CHUNK), :]
                for gl in range(GROUPS_PER_TILE):
                    shift = ((t8 - gl) * SSM_GROUP) % LANES
                    r = pltpu.roll(src, shift, 1) if shift else src
                    acc[gl] = r if t8 == 0 else jnp.where(lane_grp == t8, r, acc[gl])
            for gl in range(GROUPS_PER_TILE):
                uc_ref[jb * GROUPS_PER_TILE + gl, :, half * LANES:(half + 1) * LANES] = acc[gl].astype(BF16)


def _from_chunk_layout(yc_ref, y_s):
    n_chunks = yc_ref.shape[1]
    lane_grp = _lane_group()
    for jb in range(SSM_GROUPS // GROUPS_PER_TILE):
        acc = [None] * SSM_CHUNK
        for gl in range(GROUPS_PER_TILE):
            g = jb * GROUPS_PER_TILE + gl
            for half in range(SSM_CHUNK // CHUNKS_PER_TILE):
                c0 = (g % 2) * CHUNK_COLS + half * LANES
                src = yc_ref[g // 2, :, c0:c0 + LANES].astype(F32)
                for t8 in range(CHUNKS_PER_TILE):
                    tl = half * CHUNKS_PER_TILE + t8
                    shift = ((gl - t8) * SSM_GROUP) % LANES
                    r = pltpu.roll(src, shift, 1) if shift else src
                    acc[tl] = r if gl == 0 else jnp.where(lane_grp == gl, r, acc[tl])
        for tl in range(SSM_CHUNK):
            y_s[jb, pl.ds(tl, n_chunks, stride=SSM_CHUNK), :] = acc[tl]


def _proj_kernel(x_ref, sc_ref, sh_ref, cos_ref, sa_ref, sb_ref, w_ref,
                 q_ref, k_ref, v_ref, kb_ref, vb_ref, u_ref, ga_ref, gs_ref, *maybe_chunked):
    h = (_layer_norm(x_ref[...]) * (1.0 + sc_ref[0]) + sh_ref[0]).astype(BF16)
    cos_t, sin_a, sin_b = cos_ref[...], sa_ref[...], sb_ref[...]

    def rope(t):
        up = pltpu.roll(t, LANES - ROT_DIM // 2, 1)
        dn = pltpu.roll(t, ROT_DIM // 2, 1)
        return t * cos_t + up * sin_a + dn * sin_b

    def mm(c0, c1):
        return jnp.dot(h, w_ref[:, c0:c1], preferred_element_type=F32)

    q = mm(0, Q_COLS)
    for s in range(Q_COLS // LANES):
        sl = slice(s * LANES, (s + 1) * LANES)
        q_ref[:, sl] = (rope(q[:, sl]) * QK_SCALE).astype(BF16)
    tm = x_ref.shape[0]
    k = mm(Q_COLS, Q_COLS + K_COLS)
    c0 = Q_COLS + K_COLS
    v = mm(c0, c0 + V_COLS)
    vb_ref[...] = v.astype(BF16)
    for n in range(N_KV_HEADS):
        sl = slice(n * LANES, (n + 1) * LANES)
        kr = rope(k[:, sl])
        kb_ref[:, sl] = kr.astype(BF16)
        k_ref[pl.ds(n, tm, stride=N_KV_HEADS), :] = kr
        v_ref[pl.ds(n, tm, stride=N_KV_HEADS), :] = v[:, sl]
    c0 += V_COLS
    u = mm(c0, c0 + D_MODEL)
    u_ref[...] = u
    c0 += D_MODEL
    ga_ref[...] = mm(c0, c0 + D_MODEL).astype(BF16)
    c0 += D_MODEL
    gs_ref[...] = mm(c0, c0 + D_MODEL).astype(BF16)
    if maybe_chunked:
        uc_ref, u_s = maybe_chunked
        for jb in range(D_MODEL // LANES):
            u_s[jb] = u[:, jb * LANES:(jb + 1) * LANES]
        _to_chunk_layout(u_s, uc_ref)


def _proj(x, sc, sh, tables, w_in, *, tm, rows_per_mod, table_blocks, chunk_u):
    T = x.shape[0]
    mod_rows = sc.shape[1]
    mod_spec = pl.BlockSpec((1, mod_rows, D_MODEL), lambda i: (i * tm // rows_per_mod, 0, 0))
    tab_spec = pl.BlockSpec((tm, LANES), lambda i: (i % table_blocks, 0))
    row = lambda w: pl.BlockSpec((tm, w), lambda i: (i, 0))
    outs = [(Q_COLS, BF16), (K_COLS, F32), (V_COLS, F32), (K_COLS, BF16), (V_COLS, BF16),
            (D_MODEL, F32), (D_MODEL, BF16), (D_MODEL, BF16)]
    out_specs = [row(w) for w, _ in outs]
    out_shape = [jax.ShapeDtypeStruct((T, w), dt) for w, dt in outs]
    for idx in (1, 2):
        out_specs[idx] = pl.BlockSpec((tm * N_KV_HEADS, QK_DIM), lambda i: (i, 0))
        out_shape[idx] = jax.ShapeDtypeStruct((T * N_KV_HEADS, QK_DIM), F32)
    if chunk_u:
        out_specs.append(pl.BlockSpec((SSM_GROUPS, tm // SSM_CHUNK, CHUNK_COLS), lambda i: (0, i, 0)))
        out_shape.append(jax.ShapeDtypeStruct((SSM_GROUPS, T // SSM_CHUNK, CHUNK_COLS), BF16))
    return pl.pallas_call(
        _proj_kernel,
        grid=(T // tm,),
        in_specs=[row(D_MODEL), mod_spec, mod_spec, tab_spec, tab_spec, tab_spec,
                  _const_spec((D_MODEL, IN_COLS))],
        out_specs=out_specs,
        out_shape=out_shape,
        scratch_shapes=[pltpu.VMEM((D_MODEL // LANES, tm, LANES), F32)] if chunk_u else [],
        compiler_params=_cparams(1),
    )(x, sc, sh, *tables, w_in)


def _rope_tables(pos):
    inv = ROPE_THETA ** (-jnp.arange(0, ROT_DIM, 2, dtype=F32) / ROT_DIM)
    ang = pos.astype(F32)[:, None] * inv[None, :]
    cos, sin = jnp.cos(ang), jnp.sin(ang)
    n, half = pos.shape[0], ROT_DIM // 2
    ones = jnp.ones((n, HD - ROT_DIM), F32)
    zeros = lambda w: jnp.zeros((n, w), F32)
    cos_t = jnp.concatenate([cos, cos, ones], axis=-1)
    sin_a = jnp.concatenate([-sin, zeros(HD - half)], axis=-1)
    sin_b = jnp.concatenate([zeros(half), sin, zeros(HD - ROT_DIM)], axis=-1)
    return tuple(jnp.tile(t, (1, LANES // HD)) for t in (cos_t, sin_a, sin_b))


def _lambda(lq1, lk1, lq2, lk2):
    return (jnp.exp(jnp.sum(lq1[...] * lk1[...], axis=-1, keepdims=True))
            - jnp.exp(jnp.sum(lq2[...] * lk2[...], axis=-1, keepdims=True)) + LAM_INIT)


def _sub_norm(o, subw):
    o = o * lax.rsqrt(jnp.mean(o * o, axis=-1, keepdims=True) + LN_EPS)
    return o * subw * (1.0 - LAM_INIT)


N_MAPS = 2 * GQA_GROUP


def _attn_kernel(lq1, lk1, lq2, lk2, subw_ref, q_ref, k_ref, v_ref, o_ref,
                 q4_s, s_s, mpart_s, m_s, v1_s, acc_s, *, tq, tk):
    qi = pl.program_id(2)
    rows = N_MAPS * tq

    @pl.when(qi == 0)
    def _():
        v1_s[:, :V_DIM] = v_ref[...]
        v1_s[:, V_DIM:] = jnp.ones((v1_s.shape[0], V_DIM), BF16)

    comp0 = lax.broadcasted_iota(jnp.int32, (1, QK_DIM), 1) < HD
    for g in range(GQA_GROUP):
        qg = q_ref[:, g * QK_DIM:(g + 1) * QK_DIM].astype(F32)
        q4_s[(2 * g) * tq:(2 * g + 1) * tq, :] = jnp.where(comp0, qg, 0.0).astype(BF16)
        q4_s[(2 * g + 1) * tq:(2 * g + 2) * tq, :] = jnp.where(comp0, 0.0, qg).astype(BF16)
    n_blk = ((qi + 1) * tq - 1) // tk + 1
    mpart_s[...] = jnp.full(mpart_s.shape, -jnp.inf, F32)

    def scores(ki, masked):
        start = pl.multiple_of(ki * tk, tk)
        s = lax.dot_general(q4_s[...], k_ref[pl.ds(start, tk), :], _NT, preferred_element_type=F32)
        if masked:
            q_pos = qi * tq + lax.broadcasted_iota(jnp.int32, (tq, tk), 0)
            keep = q_pos >= start + lax.broadcasted_iota(jnp.int32, (tq, tk), 1)
            s = jnp.concatenate([jnp.where(keep, s[mp * tq:(mp + 1) * tq], -jnp.inf) for mp in range(N_MAPS)], axis=0)
        s_s[ki] = s
        m = mpart_s[...]
        for j in range(tk // LANES):
            m = jnp.maximum(m, s[:, j * LANES:(j + 1) * LANES])
        mpart_s[...] = m

    def pairs(fn, n):
        def four(i, carry):
            for r in range(4):
                fn(4 * i + r)
            return carry

        lax.fori_loop(0, n // 4, four, 0)

        @pl.when(n % 4 >= 2)
        def _():
            fn((n // 4) * 4)
            fn((n // 4) * 4 + 1)

        @pl.when(n % 2 == 1)
        def _():
            fn(n - 1)

    pairs(lambda ki: scores(ki, False), n_blk - 1)
    scores(n_blk - 1, True)

    m_s[...] = jnp.broadcast_to(jnp.max(mpart_s[...], axis=-1, keepdims=True), m_s.shape)
    acc_s[...] = jnp.zeros(acc_s.shape, F32)

    def weights(ki):
        start = pl.multiple_of(ki * tk, tk)
        s = s_s[ki]
        m = m_s[...]
        p = jnp.concatenate([jnp.exp2((s[:, j * LANES:(j + 1) * LANES] - m).astype(BF16))
                             for j in range(tk // LANES)], axis=-1)
        acc_s[...] += jnp.dot(p, v1_s[pl.ds(start, tk), :], preferred_element_type=F32)

    pairs(weights, n_blk)

    lam = _lambda(lq1, lk1, lq2, lk2)
    o = acc_s[:, :V_DIM] / acc_s[:, V_DIM:V_DIM + 1]
    for g in range(GQA_GROUP):
        d = o[(2 * g) * tq:(2 * g + 1) * tq] - lam * o[(2 * g + 1) * tq:(2 * g + 2) * tq]
        o_ref[:, g * V_DIM:(g + 1) * V_DIM] = _sub_norm(d, subw_ref[...]).astype(BF16)


def _prompt_attention(q, kb, vb, lams, subw, *, batch, seq, tq, tk):
    assert tk % tq == 0 and tq & (tq - 1) == 0
    T = batch * seq
    nq = seq // tq
    rows = N_MAPS * tq
    small = pl.BlockSpec((1, HD), lambda b, n, i: (0, 0))
    return pl.pallas_call(
        functools.partial(_attn_kernel, tq=tq, tk=tk),
        grid=(batch, N_KV_HEADS, nq),
        in_specs=[small, small, small, small,
                  pl.BlockSpec((1, V_DIM), lambda b, n, i: (0, 0)),
                  pl.BlockSpec((tq, GQA_GROUP * QK_DIM), lambda b, n, i: (b * nq + i, n)),
                  pl.BlockSpec((seq, QK_DIM), lambda b, n, i: (b, n)),
                  pl.BlockSpec((seq, V_DIM), lambda b, n, i: (b, n))],
        out_specs=pl.BlockSpec((tq, GQA_GROUP * V_DIM), lambda b, n, i: (b * nq + i, n)),
        out_shape=jax.ShapeDtypeStruct((T, N_HEADS * V_DIM), BF16),
        scratch_shapes=[pltpu.VMEM((rows, QK_DIM), BF16),
                        pltpu.VMEM((seq // tk, rows, tk), F32),
                        pltpu.VMEM((rows, LANES), F32),
                        pltpu.VMEM((rows, LANES), F32),
                        pltpu.VMEM((seq, 2 * V_DIM), BF16),
                        pltpu.VMEM((rows, 2 * V_DIM), F32)],
        compiler_params=_cparams(3),
    )(*lams, subw, q, kb, vb)


DEC_PAGES_PER_STEP = 32
DEC_ROWS = N_KV_HEADS * N_MAPS


def _decode_kernel(pt_ref, lq1, lk1, lq2, lk2, subw_ref, qa_ref, knew_ref, vnew_ref, *rest):
    del pt_ref
    npg = DEC_PAGES_PER_STEP
    k_refs, v_refs = rest[:npg], rest[npg:2 * npg]
    o_ref, m_s, l_s, acc_s = rest[2 * npg:]
    c = pl.program_id(1)
    qa = qa_ref[0]
    page_rows = k_refs[0].shape[0]

    @pl.when(c == 0)
    def _():
        m_s[...] = jnp.sum(qa.astype(F32) * knew_ref[0], axis=-1, keepdims=True)
        l_s[...] = jnp.ones(l_s.shape, F32)
        acc_s[...] = vnew_ref[0]

    own_head = (jnp.bitwise_and(lax.broadcasted_iota(jnp.int32, (DEC_ROWS, page_rows), 1), N_KV_HEADS - 1)
                == lax.broadcasted_iota(jnp.int32, (DEC_ROWS, page_rows), 0) // N_MAPS)
    s = jnp.concatenate(
        [jnp.where(own_head, lax.dot_general(qa, k_refs[j][...].astype(BF16), _NT, preferred_element_type=F32),
                   -jnp.inf) for j in range(npg)], axis=-1)
    m_prev = m_s[...]
    m_new = jnp.maximum(m_prev, jnp.max(s, axis=-1, keepdims=True))
    alpha = jnp.exp2(m_prev - m_new)
    p = jnp.exp2(s - m_new)
    l_s[...] = alpha * l_s[...] + jnp.sum(p, axis=-1, keepdims=True)
    pv = jnp.dot(p[:, :page_rows].astype(BF16), v_refs[0][...].astype(BF16), preferred_element_type=F32)
    for j in range(1, npg):
        pv += jnp.dot(p[:, j * page_rows:(j + 1) * page_rows].astype(BF16), v_refs[j][...].astype(BF16),
                      preferred_element_type=F32)
    acc_s[...] = alpha * acc_s[...] + pv
    m_s[...] = m_new

    @pl.when(c == pl.num_programs(1) - 1)
    def _():
        lam = _lambda(lq1, lk1, lq2, lk2)
        o = acc_s[...] / l_s[...]
        for n in range(N_KV_HEADS):
            r = n * N_MAPS
            d = o[r:r + GQA_GROUP] - lam * o[r + GQA_GROUP:r + 2 * GQA_GROUP]
            o_ref[0, n * GQA_GROUP:(n + 1) * GQA_GROUP, :] = _sub_norm(d, subw_ref[...])


def _decode_attention(q, k_new, v_new, cache_k, cache_v, page_table, lams, subw, layer):
    bsz = q.shape[0]
    page = cache_k.shape[2]
    n_pages = page_table.shape[1]
    npg = DEC_PAGES_PER_STEP
    q5 = q.reshape(bsz, N_KV_HEADS, GQA_GROUP, 2, HD)
    qa = jnp.einsum('bngcd,ce->bncged', q5, jnp.eye(2, dtype=q.dtype)).reshape(bsz, DEC_ROWS, QK_DIM)
    per_row = lambda t: jnp.repeat(t.reshape(bsz, N_KV_HEADS, QK_DIM), N_MAPS, axis=1)
    small = pl.BlockSpec((1, HD), lambda b, c, pt: (0, 0))
    rows_spec = pl.BlockSpec((1, DEC_ROWS, QK_DIM), lambda b, c, pt: (b, 0, 0))
    n_pool = cache_k.shape[1]
    cache_k = cache_k.reshape(-1, page * N_KV_HEADS, QK_DIM)
    cache_v = cache_v.reshape(-1, page * N_KV_HEADS, V_DIM)
    page_specs = [pl.BlockSpec((None, page * N_KV_HEADS, QK_DIM),
                               functools.partial(lambda b, c, pt, j: (layer * n_pool + pt[b, c * npg + j], 0, 0), j=j))
                  for j in range(npg)]
    grid_spec = pltpu.PrefetchScalarGridSpec(
        num_scalar_prefetch=1,
        grid=(bsz, n_pages // npg),
        in_specs=[small, small, small, small,
                  pl.BlockSpec((1, V_DIM), lambda b, c, pt: (0, 0)),
                  rows_spec, rows_spec, rows_spec] + page_specs + page_specs,
        out_specs=pl.BlockSpec((1, N_HEADS, V_DIM), lambda b, c, pt: (b, 0, 0)),
        scratch_shapes=[pltpu.VMEM((DEC_ROWS, 1), F32), pltpu.VMEM((DEC_ROWS, 1), F32),
                        pltpu.VMEM((DEC_ROWS, V_DIM), F32)])
    o = pl.pallas_call(
        _decode_kernel,
        grid_spec=grid_spec,
        out_shape=jax.ShapeDtypeStruct((bsz, N_HEADS, V_DIM), F32),
        compiler_params=_cparams(2),
    )(page_table, *lams, subw, qa, per_row(k_new), per_row(v_new), *([cache_k] * npg), *([cache_v] * npg))
    return o.reshape(bsz, N_HEADS * V_DIM)


N_POW = 24


SSM_TILE_GROUPS = 16
PAIRS_PER_TILE = SSM_TILE_GROUPS // 2


def _ssm_prep_kernel(lr_ref, li_ref, ldt_ref, cre_ref, cim_ref, bre_ref, bim_ref,
                     toep_ref, wsr_ref, wsi_ref, wcr_ref, wci_ref, a1r_ref, a1i_ref, acr_ref, aci_ref,
                     bdr_ref, bdi_ref, cdr_ref, cdi_ref):
    P, C, TL = SSM_STATE, SSM_GROUP, SSM_CHUNK
    hi = lax.Precision.HIGHEST
    ell = lax.broadcasted_iota(jnp.int32, (N_POW, 1), 0).astype(F32)
    eye = (lax.broadcasted_iota(jnp.int32, (P, P), 0) == lax.broadcasted_iota(jnp.int32, (P, P), 1)).astype(F32)
    col = lax.broadcasted_iota(jnp.int32, (1, CHUNK_COLS), 1)
    tile_slot = 2 * (pl.program_id(0) % PAIRS_PER_TILE)
    st_slot = lax.broadcasted_iota(jnp.int32, (1, SSM_TILE_GROUPS * P), 1) // P
    ch_slot = col // C
    wcr_ref[...] = jnp.zeros(wcr_ref.shape, BF16)
    wci_ref[...] = jnp.zeros(wci_ref.shape, BF16)
    for j in range(2):
        lr, li = lr_ref[j], li_ref[j]
        dt = jnp.exp(ldt_ref[j])
        mag = jnp.exp(ell * (lr * dt))
        ph = ell * (li * dt)
        p_re, p_im = mag * jnp.cos(ph), mag * jnp.sin(ph)
        a_re, a_im = p_re[1:2], p_im[1:2]
        a1r_ref[j], a1i_ref[j] = a_re, a_im
        acr_ref[j], aci_ref[j] = p_re[TL:TL + 1], p_im[TL:TL + 1]
        den = lr * lr + li * li
        nr, ni = a_re - 1.0, a_im
        co_re = (nr * lr + ni * li) / den
        co_im = (ni * lr - nr * li) / den
        b_re, b_im = bre_ref[j], bim_ref[j]
        bb_re = co_re * b_re - co_im * b_im
        bb_im = co_re * b_im + co_im * b_re
        c_re, c_im = cre_ref[j], cim_ref[j]
        g_re = jnp.concatenate([c_re * p_re[l:l + 1] - c_im * p_im[l:l + 1] for l in range(TL + 1)], axis=0)
        g_im = jnp.concatenate([c_re * p_im[l:l + 1] + c_im * p_re[l:l + 1] for l in range(TL + 1)], axis=0)
        kt = (lax.dot_general(bb_re, g_re[:CHUNK_COLS], _NT, precision=hi, preferred_element_type=F32)
              - lax.dot_general(bb_im, g_im[:CHUNK_COLS], _NT, precision=hi, preferred_element_type=F32))
        for s in range(TL):
            rows = slice(s * C, (s + 1) * C)
            shifted = pltpu.roll(kt, s * C, 1) if s else kt
            toep_ref[j, rows, :] = jnp.where(col >= s * C, shifted, 0.0).astype(BF16)
            e = TL - 1 - s
            wsr_ref[j, rows, :] = (p_re[e:e + 1] * bb_re - p_im[e:e + 1] * bb_im).astype(BF16)
            wsi_ref[j, rows, :] = (p_re[e:e + 1] * bb_im + p_im[e:e + 1] * bb_re).astype(BF16)
        blk = (slice(j * P, (j + 1) * P), slice(j * CHUNK_COLS, (j + 1) * CHUNK_COLS))
        wcr_ref[(0,) + blk] = lax.dot_general(eye, g_re[C:], _NT, precision=hi,
                                              preferred_element_type=F32).astype(BF16)
        wci_ref[(0,) + blk] = (-lax.dot_general(eye, g_im[C:], _NT, precision=hi,
                                                preferred_element_type=F32)).astype(BF16)
        mine = st_slot == tile_slot + j
        brow = slice(j * C, (j + 1) * C)
        bdr_ref[0, brow, :] = jnp.where(mine, jnp.tile(bb_re, (1, SSM_TILE_GROUPS)), 0.0).astype(BF16)
        bdi_ref[0, brow, :] = jnp.where(mine, jnp.tile(bb_im, (1, SSM_TILE_GROUPS)), 0.0).astype(BF16)
        mine_c = ch_slot == tile_slot + j
        crow = slice(j * P, (j + 1) * P)
        ct_re = lax.dot_general(eye, jnp.tile(c_re, (SSM_TILE_GROUPS, 1)), _NT, precision=hi, preferred_element_type=F32)
        ct_im = lax.dot_general(eye, jnp.tile(c_im, (SSM_TILE_GROUPS, 1)), _NT, precision=hi, preferred_element_type=F32)
        cdr_ref[0, crow, :] = jnp.where(mine_c, ct_re, 0.0).astype(BF16)
        cdi_ref[0, crow, :] = jnp.where(mine_c, -ct_im, 0.0).astype(BF16)


def _ssm_prep(lam_re, lam_im, log_dt, b_re, b_im, c_re, c_im):
    G, P, C = SSM_GROUPS, SSM_STATE, SSM_GROUP
    n_tiles = G // SSM_TILE_GROUPS
    pair = lambda *s: pl.BlockSpec((2,) + s, lambda q: (q, 0, 0))
    tile = lambda *s: pl.BlockSpec((1,) + s, lambda q: (q // PAIRS_PER_TILE, q % PAIRS_PER_TILE, 0))
    wc_spec = pl.BlockSpec((1, PAIR_STATES, 2 * CHUNK_COLS), lambda q: (q, 0, 0))
    outs = dict(
        toep=(pair(CHUNK_COLS, CHUNK_COLS), (G, CHUNK_COLS, CHUNK_COLS), BF16),
        ws_re=(pair(CHUNK_COLS, P), (G, CHUNK_COLS, P), BF16), ws_im=(pair(CHUNK_COLS, P), (G, CHUNK_COLS, P), BF16),
        wc_re=(wc_spec, (G // 2, PAIR_STATES, 2 * CHUNK_COLS), BF16),
        wc_im=(wc_spec, (G // 2, PAIR_STATES, 2 * CHUNK_COLS), BF16),
        a1_re=(pair(1, P), (G, 1, P), F32), a1_im=(pair(1, P), (G, 1, P), F32),
        ac_re=(pair(1, P), (G, 1, P), F32), ac_im=(pair(1, P), (G, 1, P), F32),
        bd_re=(tile(2 * C, SSM_TILE_GROUPS * P), (n_tiles, SSM_TILE_GROUPS * C, SSM_TILE_GROUPS * P), BF16),
        bd_im=(tile(2 * C, SSM_TILE_GROUPS * P), (n_tiles, SSM_TILE_GROUPS * C, SSM_TILE_GROUPS * P), BF16),
        cd_re=(tile(2 * P, SSM_TILE_GROUPS * C), (n_tiles, SSM_TILE_GROUPS * P, SSM_TILE_GROUPS * C), BF16),
        cd_im=(tile(2 * P, SSM_TILE_GROUPS * C), (n_tiles, SSM_TILE_GROUPS * P, SSM_TILE_GROUPS * C), BF16))
    res = pl.pallas_call(
        _ssm_prep_kernel,
        grid=(G // 2,),
        in_specs=[pair(1, P), pair(1, P), pair(1, 1), pair(C, P), pair(C, P), pair(C, P), pair(C, P)],
        out_specs=[v[0] for v in outs.values()],
        out_shape=[jax.ShapeDtypeStruct(v[1], v[2]) for v in outs.values()],
        compiler_params=_cparams(1),
    )(lam_re.reshape(G, 1, P), lam_im.reshape(G, 1, P), log_dt.reshape(G, 1, 1),
      c_re, c_im, jnp.swapaxes(b_re, 1, 2), jnp.swapaxes(b_im, 1, 2))
    return dict(zip(outs, res))


def _ssm_state_kernel(uc_ref, wsr_ref, wsi_ref, sre_ref, sim_ref):
    def both(w_ref):
        return jnp.concatenate([jnp.dot(uc_ref[j], w_ref[j], preferred_element_type=F32) for j in range(2)], axis=-1)
    sre_ref[...] = both(wsr_ref)
    sim_ref[...] = both(wsi_ref)


def _ssm_scan_kernel(ar_ref, ai_ref, sre_ref, sim_ref, hpr_ref, hpi_ref, hfr_ref, hfi_ref):
    ar, ai = ar_ref[...], ai_ref[...]
    batch, n_chunks, width = sre_ref.shape

    def body(k, carry):
        nxt = []
        for b in range(batch):
            hr, hi = carry[2 * b], carry[2 * b + 1]
            row = (b, pl.ds(k, 1), slice(None))
            hpr_ref[row] = hr
            hpi_ref[row] = hi
            nxt += [ar * hr - ai * hi + sre_ref[row], ar * hi + ai * hr + sim_ref[row]]
        return tuple(nxt)

    zero = jnp.zeros((1, width), F32)
    fin = lax.fori_loop(0, n_chunks, body, (zero,) * (2 * batch))
    for b in range(batch):
        hfr_ref[b:b + 1, :] = fin[2 * b]
        hfi_ref[b:b + 1, :] = fin[2 * b + 1]


def _ssm_out_kernel(uc_ref, toep_ref, hpr_ref, hpi_ref, wcr_ref, wci_ref, y_ref):
    y = jnp.concatenate([jnp.dot(uc_ref[j], toep_ref[j], preferred_element_type=F32) for j in range(2)], axis=-1)
    y += jnp.dot(hpr_ref[...].astype(BF16), wcr_ref[0], preferred_element_type=F32)
    y += jnp.dot(hpi_ref[...].astype(BF16), wci_ref[0], preferred_element_type=F32)
    y_ref[0] = y.astype(BF16)


def _prompt_ssm(uc, prep, *, batch, seq):
    G, P = SSM_GROUPS, SSM_STATE
    nk = seq // SSM_CHUNK
    rows = nk * batch

    pair3 = lambda a, b: pl.BlockSpec((2, a, b), lambda q: (q, 0, 0))
    s_spec = pl.BlockSpec((rows, PAIR_STATES), lambda q: (0, q))
    s_re, s_im = pl.pallas_call(
        _ssm_state_kernel,
        grid=(G // 2,),
        in_specs=[pair3(rows, CHUNK_COLS), pair3(CHUNK_COLS, P), pair3(CHUNK_COLS, P)],
        out_specs=[s_spec, s_spec],
        out_shape=[jax.ShapeDtypeStruct((rows, G * P), F32)] * 2,
        compiler_params=_cparams(1),
    )(uc, prep['ws_re'], prep['ws_im'])

    wb = 512
    a_spec = pl.BlockSpec((1, wb), lambda j: (0, j))
    seq_spec = pl.BlockSpec((batch, nk, wb), lambda j: (0, 0, j))
    fin_spec = pl.BlockSpec((batch, wb), lambda j: (0, j))
    hp_re, hp_im, hf_re, hf_im = pl.pallas_call(
        _ssm_scan_kernel,
        grid=(G * P // wb,),
        in_specs=[a_spec, a_spec, seq_spec, seq_spec],
        out_specs=[seq_spec, seq_spec, fin_spec, fin_spec],
        out_shape=[jax.ShapeDtypeStruct((batch, nk, G * P), F32)] * 2 + [jax.ShapeDtypeStruct((batch, G * P), F32)] * 2,
        compiler_params=_cparams(1),
    )(prep['ac_re'].reshape(1, G * P), prep['ac_im'].reshape(1, G * P),
      s_re.reshape(batch, nk, G * P), s_im.reshape(batch, nk, G * P))

    yc = pl.pallas_call(
        _ssm_out_kernel,
        grid=(G // 2,),
        in_specs=[pair3(rows, CHUNK_COLS), pair3(CHUNK_COLS, CHUNK_COLS), s_spec, s_spec,
                  pl.BlockSpec((1, PAIR_STATES, 2 * CHUNK_COLS), lambda q: (q, 0, 0)),
                  pl.BlockSpec((1, PAIR_STATES, 2 * CHUNK_COLS), lambda q: (q, 0, 0))],
        out_specs=pl.BlockSpec((1, rows, 2 * CHUNK_COLS), lambda q: (q, 0, 0)),
        out_shape=jax.ShapeDtypeStruct((G // 2, rows, 2 * CHUNK_COLS), BF16),
        compiler_params=_cparams(1),
    )(uc, prep['toep'], hp_re.reshape(rows, G * P), hp_im.reshape(rows, G * P), prep['wc_re'], prep['wc_im'])
    return yc, hf_re, hf_im


def _ssm_step_kernel(u_ref, h0r_ref, h0i_ref, ar_ref, ai_ref, bdr_ref, bdi_ref, cdr_ref, cdi_ref,
                     hr_ref, hi_ref, y_ref):
    ch = SSM_TILE_GROUPS * SSM_GROUP
    st = SSM_TILE_GROUPS * SSM_STATE
    for t in range(SSM_GROUPS // SSM_TILE_GROUPS):
        cs, ss = slice(t * ch, (t + 1) * ch), slice(t * st, (t + 1) * st)
        ub = u_ref[:, cs].astype(BF16)
        ar, ai, h0r, h0i = ar_ref[:, ss], ai_ref[:, ss], h0r_ref[:, ss], h0i_ref[:, ss]
        hr = ar * h0r - ai * h0i + jnp.dot(ub, bdr_ref[t], preferred_element_type=F32)
        hi = ar * h0i + ai * h0r + jnp.dot(ub, bdi_ref[t], preferred_element_type=F32)
        hr_ref[:, ss] = hr
        hi_ref[:, ss] = hi
        y_ref[:, cs] = (jnp.dot(hr.astype(BF16), cdr_ref[t], preferred_element_type=F32)
                        + jnp.dot(hi.astype(BF16), cdi_ref[t], preferred_element_type=F32))


def _sample_ssm(u, h0_re, h0_im, prep):
    bsz = u.shape[0]
    n_st = SSM_STATES
    args = (u, h0_re.reshape(bsz, n_st), h0_im.reshape(bsz, n_st),
            prep['a1_re'].reshape(1, n_st), prep['a1_im'].reshape(1, n_st),
            prep['bd_re'], prep['bd_im'], prep['cd_re'], prep['cd_im'])
    return pl.pallas_call(
        _ssm_step_kernel,
        grid=(1,),
        in_specs=[_const_spec(a.shape) for a in args],
        out_specs=[pl.BlockSpec((bsz, n_st), lambda i: (0, 0))] * 2 + [pl.BlockSpec((bsz, D_MODEL), lambda i: (0, 0))],
        out_shape=[jax.ShapeDtypeStruct((bsz, n_st), F32)] * 2 + [jax.ShapeDtypeStruct((bsz, D_MODEL), F32)],
        compiler_params=_cparams(1),
    )(*args)


FF_CHUNK = D_FF // 4


def _merge_kernel(x_ref, o_ref, ga_ref, gs_ref, y_ref, u_ref, g1_ref, sc2_ref, sh2_ref, g2_ref,
                  d_ref, wglu_ref, bglu_ref, wo_ref, ln1g_ref, ln1b_ref,
                  wup_ref, bup_ref, wdn_ref, bdn_ref, ln2g_ref, ln2b_ref, out_ref, x1_s, h2_s, *maybe_y_s):
    i = pl.program_id(0)
    slot = i % 2

    @pl.when(i == 0)
    def _():
        x1_s[1] = jnp.zeros(x1_s.shape[1:], F32)
        h2_s[1] = jnp.zeros(h2_s.shape[1:], BF16)

    x1p = x1_s[1 - slot]
    h2p = h2_s[1 - slot]

    def mlp_chunk(c):
        cs = slice(c * FF_CHUNK, (c + 1) * FF_CHUNK)
        a = jnp.maximum(jnp.dot(h2p, wup_ref[:, cs], preferred_element_type=F32) + bup_ref[:, cs], 0.0)
        return jnp.dot((a * a).astype(BF16), wdn_ref[cs, :], preferred_element_type=F32)

    f = mlp_chunk(0)
    if maybe_y_s:
        y_s = maybe_y_s[0]
        _from_chunk_layout(y_ref, y_s)
        y = jnp.concatenate([y_s[jb] for jb in range(D_MODEL // LANES)], axis=-1)
    else:
        y = y_ref[...]
    yd = y + d_ref[...] * u_ref[...]
    g = jax.nn.gelu(yd, approximate=True)
    f += mlp_chunk(1)
    z = jnp.dot(g.astype(BF16), wglu_ref[...], preferred_element_type=F32) + bglu_ref[...]
    y_ssm = g * jax.nn.sigmoid(z)
    mixed = (jax.nn.sigmoid(ga_ref[...].astype(F32)) * o_ref[...].astype(F32)
             + jax.nn.sigmoid(gs_ref[...].astype(F32)) * y_ssm)
    f += mlp_chunk(2)
    t = jnp.dot(mixed.astype(BF16), wo_ref[...], preferred_element_type=F32)
    x1 = _layer_norm(ALPHA * x_ref[...] + (1.0 + g1_ref[0]) * t) * ln1g_ref[...] + ln1b_ref[...]
    x1_s[slot] = x1
    f += mlp_chunk(3) + bdn_ref[...]
    h2_s[slot] = (_layer_norm(x1) * (1.0 + sc2_ref[0]) + sh2_ref[0]).astype(BF16)
    out_ref[...] = _layer_norm(ALPHA * x1p + (1.0 + g2_ref[0]) * f) * ln2g_ref[...] + ln2b_ref[...]


def _merge(x, o, ga, gs, y, u, mods, params, *, tm, rows_per_mod):
    T = x.shape[0]
    n = T // tm
    mod_rows = mods[0].shape[1]
    cur = lambda i: jnp.minimum(i, n - 1)
    prev = lambda i: jnp.maximum(i - 1, 0)
    mod_cur = pl.BlockSpec((1, mod_rows, D_MODEL), lambda i: (cur(i) * tm // rows_per_mod, 0, 0))
    mod_prev = pl.BlockSpec((1, mod_rows, D_MODEL), lambda i: (prev(i) * tm // rows_per_mod, 0, 0))
    row = pl.BlockSpec((tm, D_MODEL), lambda i: (cur(i), 0))
    chunked = y.ndim == 3
    y_spec = (pl.BlockSpec((SSM_GROUPS // 2, tm // SSM_CHUNK, 2 * CHUNK_COLS), lambda i: (0, cur(i), 0))
              if chunked else row)
    scratch = [pltpu.VMEM((2, tm, D_MODEL), F32), pltpu.VMEM((2, tm, D_MODEL), BF16)]
    if chunked:
        scratch.append(pltpu.VMEM((D_MODEL // LANES, tm, LANES), F32))
    return pl.pallas_call(
        _merge_kernel,
        grid=(n + 1,),
        in_specs=[row] * 4 + [y_spec, row] + [mod_cur] * 3 + [mod_prev] + [_const_spec(p.shape) for p in params],
        out_specs=pl.BlockSpec((tm, D_MODEL), lambda i: (prev(i), 0)),
        out_shape=jax.ShapeDtypeStruct((T, D_MODEL), F32),
        scratch_shapes=scratch,
        compiler_params=_cparams(1),
    )(x, o, ga, gs, y, u, *mods, *params)


def _split_mods(ada):
    return [ada[..., i * D_MODEL:(i + 1) * D_MODEL] for i in range(6)]


def kernel(x_prompt, x_sample, cache_k, cache_v, state_ssm_re, state_ssm_im, page_table, c_prompt, c_sample, w_in, lambda_q1, lambda_k1, lambda_q2, lambda_k2, subln_w, ssm_lambda_re, ssm_lambda_im, ssm_log_dt, ssm_b_re, ssm_b_im, ssm_c_re, ssm_c_im, ssm_d, w_glu, b_glu, w_o, w_ada, b_ada, ln1_g, ln1_b, ln2_g, ln2_b, w_up, b_up, w_down, b_down):
    assert w_in.shape[0] == DEPTH and x_sample.shape[1] == 1
    batch, seq, _ = x_prompt.shape
    dec_b = x_sample.shape[0]
    page = cache_k.shape[2]
    past_len = page_table.shape[1] * page
    l = 0
    row2 = lambda a: a[l].reshape(1, -1)

    pad = (-batch) % 8
    c_all = jnp.concatenate([c_prompt, jnp.zeros((pad, D_MODEL), F32), c_sample], axis=0)
    ada = _ada(c_all, w_ada[l], row2(b_ada))
    mods_p = _split_mods(ada[:batch].reshape(batch, 1, 6 * D_MODEL))
    mods_s = _split_mods(ada[batch + pad:].reshape(1, dec_b, 6 * D_MODEL))

    w_in_b = w_in[l].astype(BF16)
    lams = (row2(lambda_q1), row2(lambda_k1), row2(lambda_q2), row2(lambda_k2))
    subw = row2(subln_w)
    merge_params = (row2(ssm_d), w_glu[l].astype(BF16), row2(b_glu), w_o[l].astype(BF16), row2(ln1_g), row2(ln1_b),
                    w_up[l].astype(BF16), row2(b_up), w_down[l].astype(BF16), row2(b_down), row2(ln2_g), row2(ln2_b))
    prep = _ssm_prep(ssm_lambda_re[l], ssm_lambda_im[l], ssm_log_dt[l], ssm_b_re[l], ssm_b_im[l],
                     ssm_c_re[l], ssm_c_im[l])

    tm = 512
    T = batch * seq
    xp = x_prompt.reshape(T, D_MODEL)
    q, k, v, kb, vb, u, ga, gs, uc = _proj(xp, mods_p[1], mods_p[0], _rope_tables(jnp.arange(seq)), w_in_b,
                                           tm=tm, rows_per_mod=seq, table_blocks=seq // tm, chunk_u=True)
    o = _prompt_attention(q, kb, vb, lams, subw, batch=batch, seq=seq, tq=512, tk=512)
    yc, hf_re, hf_im = _prompt_ssm(uc, prep, batch=batch, seq=seq)
    yp = _merge(xp, o, ga, gs, yc, u, (mods_p[2], mods_p[4], mods_p[3], mods_p[5]), merge_params,
                tm=256, rows_per_mod=seq)

    xs = x_sample.reshape(dec_b, D_MODEL)
    pos_s = jnp.full((dec_b,), past_len, jnp.int32)
    qs, ks, vs, _, _, us, gas, gss = _proj(xs, mods_s[1], mods_s[0], _rope_tables(pos_s), w_in_b,
                                           tm=dec_b, rows_per_mod=dec_b, table_blocks=1, chunk_u=False)
    os_ = _decode_attention(qs, ks, vs, cache_k, cache_v, page_table, lams, subw, l).astype(BF16)
    hs_re, hs_im, y_s = _sample_ssm(us, state_ssm_re[l], state_ssm_im[l], prep)
    ys = _merge(xs, os_, gas, gss, y_s, us, (mods_s[2], mods_s[4], mods_s[3], mods_s[5]), merge_params,
                tm=dec_b, rows_per_mod=dec_b)

    st = (SSM_GROUPS, SSM_STATE)
    return (yp.reshape(batch, seq, D_MODEL), ys.reshape(dec_b, 1, D_MODEL),
            k.reshape(1, batch, seq, N_KV_HEADS, QK_DIM), v.reshape(1, batch, seq, N_KV_HEADS, V_DIM),
            hf_re.reshape(1, batch, *st), hf_im.reshape(1, batch, *st),
            ks.reshape(1, dec_b, 1, N_KV_HEADS, QK_DIM), vs.reshape(1, dec_b, 1, N_KV_HEADS, V_DIM),
            hs_re.reshape(1, dec_b, *st), hs_im.reshape(1, dec_b, *st))
```

```python
import functools
import math

import jax
import jax.numpy as jnp
from jax import lax
from jax.experimental import pallas as pl
from jax.experimental.pallas import tpu as pltpu

F32 = jnp.float32
BF16 = jnp.bfloat16

D_MODEL = 1024
N_HEADS = 8
N_KV_HEADS = 4
GQA_GROUP = N_HEADS // N_KV_HEADS
HD = 64
QK_DIM = 2 * HD
V_DIM = 2 * HD
ROT_DIM = HD // 4
ROPE_THETA = 500000.0
SSM_GROUP = 16
SSM_GROUPS = D_MODEL // SSM_GROUP
SSM_STATE = 64
SSM_STATES = SSM_GROUPS * SSM_STATE
D_FF = 4 * D_MODEL
DEPTH = 1
ALPHA = (2.0 * DEPTH) ** 0.25
LN_EPS = 1e-5
LAM_INIT = 0.8 - 0.6 * math.exp(-0.3 * 0)
Q_COLS = N_HEADS * QK_DIM
K_COLS = N_KV_HEADS * QK_DIM
V_COLS = N_KV_HEADS * V_DIM
IN_COLS = Q_COLS + K_COLS + V_COLS + 3 * D_MODEL

LANES = 128
SSM_CHUNK = 16
CHUNK_COLS = SSM_CHUNK * SSM_GROUP
PAIR_STATES = 2 * SSM_STATE
VMEM_LIMIT = 48 * 1024 * 1024

_NT = (((1,), (1,)), ((), ()))
QK_SCALE = math.log2(math.e) / math.sqrt(HD)


def _cparams(n_axes):
    return pltpu.CompilerParams(dimension_semantics=("arbitrary",) * n_axes, vmem_limit_bytes=VMEM_LIMIT)


def _const_spec(shape):
    nd = len(shape)
    return pl.BlockSpec(shape, lambda *_: (0,) * nd, pipeline_mode=pl.Buffered(1))


def _layer_norm(x):
    mu = jnp.mean(x, axis=-1, keepdims=True)
    xc = x - mu
    var = jnp.mean(xc * xc, axis=-1, keepdims=True)
    return xc * lax.rsqrt(var + LN_EPS)


def _ada_kernel(c_ref, w_ref, b_ref, o_ref):
    c = c_ref[...]
    s = c * jax.nn.sigmoid(c)
    o_ref[...] = jnp.dot(s.astype(BF16), w_ref[...].astype(BF16), preferred_element_type=F32) + b_ref[...]


def _ada(c_all, w_ada, b_ada):
    rows = c_all.shape[0]
    tn = D_MODEL
    return pl.pallas_call(
        _ada_kernel,
        grid=(6 * D_MODEL // tn,),
        in_specs=[pl.BlockSpec((rows, D_MODEL), lambda j: (0, 0)),
                  pl.BlockSpec((D_MODEL, tn), lambda j: (0, j)),
                  pl.BlockSpec((1, tn), lambda j: (0, j))],
        out_specs=pl.BlockSpec((rows, tn), lambda j: (0, j)),
        out_shape=jax.ShapeDtypeStruct((rows, 6 * D_MODEL), F32),
        compiler_params=_cparams(1),
    )(c_all, w_ada, b_ada)


GROUPS_PER_TILE = LANES // SSM_GROUP
CHUNKS_PER_TILE = LANES // SSM_GROUP


def _lane_group():
    return lax.broadcasted_iota(jnp.int32, (1, LANES), 1) // SSM_GROUP


def _to_chunk_layout(u_s, uc_ref):
    n_chunks = u_s.shape[1] // SSM_CHUNK
    lane_grp = _lane_group()
    for jb in range(SSM_GROUPS // GROUPS_PER_TILE):
        for half in range(SSM_CHUNK // CHUNKS_PER_TILE):
            acc = [None] * GROUPS_PER_TILE
            for t8 in range(CHUNKS_PER_TILE):
                src = u_s[jb, pl.ds(half * CHUNKS_PER_TILE + t8, n_chunks, stride=SSM_CHUNK), :]
                for gl in range(GROUPS_PER_TILE):
                    shift = ((t8 - gl) * SSM_GROUP) % LANES
                    r = pltpu.roll(src, shift, 1) if shift else src
                    acc[gl] = r if t8 == 0 else jnp.where(lane_grp == t8, r, acc[gl])
            for gl in range(GROUPS_PER_TILE):
                uc_ref[jb * GROUPS_PER_TILE + gl, :, half * LANES:(half + 1) * LANES] = acc[gl].astype(BF16)


def _from_chunk_layout(yc_ref, y_s):
    n_chunks = yc_ref.shape[1]
    lane_grp = _lane_group()
    for jb in range(SSM_GROUPS // GROUPS_PER_TILE):
        acc = [None] * SSM_CHUNK
        for gl in range(GROUPS_PER_TILE):
            g = jb * GROUPS_PER_TILE + gl
            for half in range(SSM_CHUNK // CHUNKS_PER_TILE):
                c0 = (g % 2) * CHUNK_COLS + half * LANES
                src = yc_ref[g // 2, :, c0:c0 + LANES].astype(F32)
                for t8 in range(CHUNKS_PER_TILE):
                    tl = half * CHUNKS_PER_TILE + t8
                    shift = ((gl - t8) * SSM_GROUP) % LANES
                    r = pltpu.roll(src, shift, 1) if shift else src
                    acc[tl] = r if gl == 0 else jnp.where(lane_grp == gl, r, acc[tl])
        for tl in range(SSM_CHUNK):
            y_s[jb, pl.ds(tl, n_chunks, stride=SSM_CHUNK), :] = acc[tl]


def _proj_kernel(x_ref, sc_ref, sh_ref, cos_ref, sa_ref, sb_ref, w_ref,
                 q_ref, k_ref, v_ref, kb_ref, vb_ref, u_ref, ga_ref, gs_ref, *maybe_chunked):
    h = (_layer_norm(x_ref[...]) * (1.0 + sc_ref[0]) + sh_ref[0]).astype(BF16)
    cos_t, sin_a, sin_b = cos_ref[...], sa_ref[...], sb_ref[...]

    def rope(t):
        up = pltpu.roll(t, LANES - ROT_DIM // 2, 1)
        dn = pltpu.roll(t, ROT_DIM // 2, 1)
        return t * cos_t + up * sin_a + dn * sin_b

    def mm(c0, c1):
        return jnp.dot(h, w_ref[:, c0:c1], preferred_element_type=F32)

    q = mm(0, Q_COLS)
    for s in range(Q_COLS // LANES):
        sl = slice(s * LANES, (s + 1) * LANES)
        q_ref[:, sl] = (rope(q[:, sl]) * QK_SCALE).astype(BF16)
    tm = x_ref.shape[0]
    k = mm(Q_COLS, Q_COLS + K_COLS)
    c0 = Q_COLS + K_COLS
    v = mm(c0, c0 + V_COLS)
    vb_ref[...] = v.astype(BF16)
    for n in range(N_KV_HEADS):
        sl = slice(n * LANES, (n + 1) * LANES)
        kr = rope(k[:, sl])
        kb_ref[:, sl] = kr.astype(BF16)
        k_ref[pl.ds(n, tm, stride=N_KV_HEADS), :] = kr
        v_ref[pl.ds(n, tm, stride=N_KV_HEADS), :] = v[:, sl]
    c0 += V_COLS
    u = mm(c0, c0 + D_MODEL)
    u_ref[...] = u
    c0 += D_MODEL
    ga_ref[...] = mm(c0, c0 + D_MODEL).astype(BF16)
    c0 += D_MODEL
    gs_ref[...] = mm(c0, c0 + D_MODEL).astype(BF16)
    if maybe_chunked:
        uc_ref, u_s = maybe_chunked
        for jb in range(D_MODEL // LANES):
            u_s[jb] = u[:, jb * LANES:(jb + 1) * LANES]
        _to_chunk_layout(u_s, uc_ref)


def _proj(x, sc, sh, tables, w_in, *, tm, rows_per_mod, table_blocks, chunk_u):
    T = x.shape[0]
    mod_rows = sc.shape[1]
    mod_spec = pl.BlockSpec((1, mod_rows, D_MODEL), lambda i: (i * tm // rows_per_mod, 0, 0))
    tab_spec = pl.BlockSpec((tm, LANES), lambda i: (i % table_blocks, 0))
    row = lambda w: pl.BlockSpec((tm, w), lambda i: (i, 0))
    outs = [(Q_COLS, BF16), (K_COLS, F32), (V_COLS, F32), (K_COLS, BF16), (V_COLS, BF16),
            (D_MODEL, F32), (D_MODEL, BF16), (D_MODEL, BF16)]
    out_specs = [row(w) for w, _ in outs]
    out_shape = [jax.ShapeDtypeStruct((T, w), dt) for w, dt in outs]
    for idx in (1, 2):
        out_specs[idx] = pl.BlockSpec((tm * N_KV_HEADS, QK_DIM), lambda i: (i, 0))
        out_shape[idx] = jax.ShapeDtypeStruct((T * N_KV_HEADS, QK_DIM), F32)
    if chunk_u:
        out_specs.append(pl.BlockSpec((SSM_GROUPS, tm // SSM_CHUNK, CHUNK_COLS), lambda i: (0, i, 0)))
        out_shape.append(jax.ShapeDtypeStruct((SSM_GROUPS, T // SSM_CHUNK, CHUNK_COLS), BF16))
    return pl.pallas_call(
        _proj_kernel,
        grid=(T // tm,),
        in_specs=[row(D_MODEL), mod_spec, mod_spec, tab_spec, tab_spec, tab_spec,
                  _const_spec((D_MODEL, IN_COLS))],
        out_specs=out_specs,
        out_shape=out_shape,
        scratch_shapes=[pltpu.VMEM((D_MODEL // LANES, tm, LANES), F32)] if chunk_u else [],
        compiler_params=_cparams(1),
    )(x, sc, sh, *tables, w_in)


def _rope_tables(pos):
    inv = ROPE_THETA ** (-jnp.arange(0, ROT_DIM, 2, dtype=F32) / ROT_DIM)
    ang = pos.astype(F32)[:, None] * inv[None, :]
    cos, sin = jnp.cos(ang), jnp.sin(ang)
    n, half = pos.shape[0], ROT_DIM // 2
    ones = jnp.ones((n, HD - ROT_DIM), F32)
    zeros = lambda w: jnp.zeros((n, w), F32)
    cos_t = jnp.concatenate([cos, cos, ones], axis=-1)
    sin_a = jnp.concatenate([-sin, zeros(HD - half)], axis=-1)
    sin_b = jnp.concatenate([zeros(half), sin, zeros(HD - ROT_DIM)], axis=-1)
    return tuple(jnp.tile(t, (1, LANES // HD)) for t in (cos_t, sin_a, sin_b))


def _lambda(lq1, lk1, lq2, lk2):
    return (jnp.exp(jnp.sum(lq1[...] * lk1[...], axis=-1, keepdims=True))
            - jnp.exp(jnp.sum(lq2[...] * lk2[...], axis=-1, keepdims=True)) + LAM_INIT)


def _sub_norm(o, subw):
    o = o * lax.rsqrt(jnp.mean(o * o, axis=-1, keepdims=True) + LN_EPS)
    return o * subw * (1.0 - LAM_INIT)


N_MAPS = 2 * GQA_GROUP


def _attn_kernel(lq1, lk1, lq2, lk2, subw_ref, q_ref, k_ref, v_ref, o_ref,
                 q4_s, s_s, mpart_s, m_s, v1_s, acc_s, *, tq, tk):
    qi = pl.program_id(2)
    rows = N_MAPS * tq

    @pl.when(qi == 0)
    def _():
        v1_s[:, :V_DIM] = v_ref[...]
        v1_s[:, V_DIM:] = jnp.ones((v1_s.shape[0], V_DIM), BF16)

    comp0 = lax.broadcasted_iota(jnp.int32, (1, QK_DIM), 1) < HD
    for g in range(GQA_GROUP):
        qg = q_ref[:, g * QK_DIM:(g + 1) * QK_DIM].astype(F32)
        q4_s[(2 * g) * tq:(2 * g + 1) * tq, :] = jnp.where(comp0, qg, 0.0).astype(BF16)
        q4_s[(2 * g + 1) * tq:(2 * g + 2) * tq, :] = jnp.where(comp0, 0.0, qg).astype(BF16)
    n_blk = ((qi + 1) * tq - 1) // tk + 1
    mpart_s[...] = jnp.full(mpart_s.shape, -jnp.inf, F32)

    def scores(ki, masked):
        start = pl.multiple_of(ki * tk, tk)
        s = lax.dot_general(q4_s[...], k_ref[pl.ds(start, tk), :], _NT, preferred_element_type=F32)
        if masked:
            q_pos = qi * tq + lax.broadcasted_iota(jnp.int32, (tq, tk), 0)
            keep = q_pos >= start + lax.broadcasted_iota(jnp.int32, (tq, tk), 1)
            s = jnp.concatenate([jnp.where(keep, s[mp * tq:(mp + 1) * tq], -jnp.inf) for mp in range(N_MAPS)], axis=0)
        s_s[ki] = s
        m = mpart_s[...]
        for j in range(tk // LANES):
            m = jnp.maximum(m, s[:, j * LANES:(j + 1) * LANES])
        mpart_s[...] = m

    def pairs(fn, n):
        def four(i, carry):
            for r in range(4):
                fn(4 * i + r)
            return carry

        lax.fori_loop(0, n // 4, four, 0)

        @pl.when(n % 4 >= 2)
        def _():
            fn((n // 4) * 4)
            fn((n // 4) * 4 + 1)

        @pl.when(n % 2 == 1)
        def _():
            fn(n - 1)

    pairs(lambda ki: scores(ki, False), n_blk - 1)
    scores(n_blk - 1, True)

    m_s[...] = jnp.broadcast_to(jnp.max(mpart_s[...], axis=-1, keepdims=True), m_s.shape)
    acc_s[...] = jnp.zeros(acc_s.shape, F32)

    def weights(ki):
        start = pl.multiple_of(ki * tk, tk)
        s = s_s[ki]
        m = m_s[...]
        p = jnp.concatenate([jnp.exp2((s[:, j * LANES:(j + 1) * LANES] - m).astype(BF16))
                             for j in range(tk // LANES)], axis=-1)
        acc_s[...] += jnp.dot(p, v1_s[pl.ds(start, tk), :], preferred_element_type=F32)

    pairs(weights, n_blk)

    lam = _lambda(lq1, lk1, lq2, lk2)
    o = acc_s[:, :V_DIM] / acc_s[:, V_DIM:V_DIM + 1]
    for g in range(GQA_GROUP):
        d = o[(2 * g) * tq:(2 * g + 1) * tq] - lam * o[(2 * g + 1) * tq:(2 * g + 2) * tq]
        o_ref[:, g * V_DIM:(g + 1) * V_DIM] = _sub_norm(d, subw_ref[...]).astype(BF16)


def _prompt_attention(q, kb, vb, lams, subw, *, batch, seq, tq, tk):
    assert tk % tq == 0 and tq & (tq - 1) == 0
    T = batch * seq
    nq = seq // tq
    rows = N_MAPS * tq
    small = pl.BlockSpec((1, HD), lambda b, n, i: (0, 0))
    return pl.pallas_call(
        functools.partial(_attn_kernel, tq=tq, tk=tk),
        grid=(batch, N_KV_HEADS, nq),
        in_specs=[small, small, small, small,
                  pl.BlockSpec((1, V_DIM), lambda b, n, i: (0, 0)),
                  pl.BlockSpec((tq, GQA_GROUP * QK_DIM), lambda b, n, i: (b * nq + i, n)),
                  pl.BlockSpec((seq, QK_DIM), lambda b, n, i: (b, n)),
                  pl.BlockSpec((seq, V_DIM), lambda b, n, i: (b, n))],
        out_specs=pl.BlockSpec((tq, GQA_GROUP * V_DIM), lambda b, n, i: (b * nq + i, n)),
        out_shape=jax.ShapeDtypeStruct((T, N_HEADS * V_DIM), BF16),
        scratch_shapes=[pltpu.VMEM((rows, QK_DIM), BF16),
                        pltpu.VMEM((seq // tk, rows, tk), F32),
                        pltpu.VMEM((rows, LANES), F32),
                        pltpu.VMEM((rows, LANES), F32),
                        pltpu.VMEM((seq, 2 * V_DIM), BF16),
                        pltpu.VMEM((rows, 2 * V_DIM), F32)],
        compiler_params=_cparams(3),
    )(*lams, subw, q, kb, vb)


DEC_PAGES_PER_STEP = 32
DEC_ROWS = N_KV_HEADS * N_MAPS


def _decode_kernel(pt_ref, lq1, lk1, lq2, lk2, subw_ref, qa_ref, knew_ref, vnew_ref, *rest):
    del pt_ref
    npg = DEC_PAGES_PER_STEP
    k_refs, v_refs = rest[:npg], rest[npg:2 * npg]
    o_ref, m_s, l_s, acc_s = rest[2 * npg:]
    c = pl.program_id(1)
    qa = qa_ref[0]
    page_rows = k_refs[0].shape[0]

    @pl.when(c == 0)
    def _():
        m_s[...] = jnp.sum(qa.astype(F32) * knew_ref[0], axis=-1, keepdims=True)
        l_s[...] = jnp.ones(l_s.shape, F32)
        acc_s[...] = vnew_ref[0]

    own_head = (jnp.bitwise_and(lax.broadcasted_iota(jnp.int32, (DEC_ROWS, page_rows), 1), N_KV_HEADS - 1)
                == lax.broadcasted_iota(jnp.int32, (DEC_ROWS, page_rows), 0) // N_MAPS)
    s = jnp.concatenate(
        [jnp.where(own_head, lax.dot_general(qa, k_refs[j][...].astype(BF16), _NT, preferred_element_type=F32),
                   -jnp.inf) for j in range(npg)], axis=-1)
    m_prev = m_s[...]
    m_new = jnp.maximum(m_prev, jnp.max(s, axis=-1, keepdims=True))
    alpha = jnp.exp2(m_prev - m_new)
    p = jnp.exp2(s - m_new)
    l_s[...] = alpha * l_s[...] + jnp.sum(p, axis=-1, keepdims=True)
    pv = jnp.dot(p[:, :page_rows].astype(BF16), v_refs[0][...].astype(BF16), preferred_element_type=F32)
    for j in range(1, npg):
        pv += jnp.dot(p[:, j * page_rows:(j + 1) * page_rows].astype(BF16), v_refs[j][...].astype(BF16),
                      preferred_element_type=F32)
    acc_s[...] = alpha * acc_s[...] + pv
    m_s[...] = m_new

    @pl.when(c == pl.num_programs(1) - 1)
    def _():
        lam = _lambda(lq1, lk1, lq2, lk2)
        o = acc_s[...] / l_s[...]
        for n in range(N_KV_HEADS):
            r = n * N_MAPS
            d = o[r:r + GQA_GROUP] - lam * o[r + GQA_GROUP:r + 2 * GQA_GROUP]
            o_ref[0, n * GQA_GROUP:(n + 1) * GQA_GROUP, :] = _sub_norm(d, subw_ref[...])


def _decode_attention(q, k_new, v_new, cache_k, cache_v, page_table, lams, subw, layer):
    bsz = q.shape[0]
    page = cache_k.shape[2]
    n_pages = page_table.shape[1]
    npg = DEC_PAGES_PER_STEP
    q5 = q.reshape(bsz, N_KV_HEADS, GQA_GROUP, 2, HD)
    qa = jnp.einsum('bngcd,ce->bncged', q5, jnp.eye(2, dtype=q.dtype)).reshape(bsz, DEC_ROWS, QK_DIM)
    per_row = lambda t: jnp.repeat(t.reshape(bsz, N_KV_HEADS, QK_DIM), N_MAPS, axis=1)
    small = pl.BlockSpec((1, HD), lambda b, c, pt: (0, 0))
    rows_spec = pl.BlockSpec((1, DEC_ROWS, QK_DIM), lambda b, c, pt: (b, 0, 0))
    n_pool = cache_k.shape[1]
    cache_k = cache_k.reshape(-1, page * N_KV_HEADS, QK_DIM)
    cache_v = cache_v.reshape(-1, page * N_KV_HEADS, V_DIM)
    page_specs = [pl.BlockSpec((None, page * N_KV_HEADS, QK_DIM),
                               functools.partial(lambda b, c, pt, j: (layer * n_pool + pt[b, c * npg + j], 0, 0), j=j))
                  for j in range(npg)]
    grid_spec = pltpu.PrefetchScalarGridSpec(
        num_scalar_prefetch=1,
        grid=(bsz, n_pages // npg),
        in_specs=[small, small, small, small,
                  pl.BlockSpec((1, V_DIM), lambda b, c, pt: (0, 0)),
                  rows_spec, rows_spec, rows_spec] + page_specs + page_specs,
        out_specs=pl.BlockSpec((1, N_HEADS, V_DIM), lambda b, c, pt: (b, 0, 0)),
        scratch_shapes=[pltpu.VMEM((DEC_ROWS, 1), F32), pltpu.VMEM((DEC_ROWS, 1), F32),
                        pltpu.VMEM((DEC_ROWS, V_DIM), F32)])
    o = pl.pallas_call(
        _decode_kernel,
        grid_spec=grid_spec,
        out_shape=jax.ShapeDtypeStruct((bsz, N_HEADS, V_DIM), F32),
        compiler_params=_cparams(2),
    )(page_table, *lams, subw, qa, per_row(k_new), per_row(v_new), *([cache_k] * npg), *([cache_v] * npg))
    return o.reshape(bsz, N_HEADS * V_DIM)


N_POW = 24


SSM_TILE_GROUPS = 16
PAIRS_PER_TILE = SSM_TILE_GROUPS // 2


def _ssm_prep_kernel(lr_ref, li_ref, ldt_ref, cre_ref, cim_ref, bre_ref, bim_ref,
                     toep_ref, wsr_ref, wsi_ref, wcr_ref, wci_ref, a1r_ref, a1i_ref, acr_ref, aci_ref,
                     bdr_ref, bdi_ref, cdr_ref, cdi_ref):
    P, C, TL = SSM_STATE, SSM_GROUP, SSM_CHUNK
    hi = lax.Precision.HIGHEST
    ell = lax.broadcasted_iota(jnp.int32, (N_POW, 1), 0).astype(F32)
    eye = (lax.broadcasted_iota(jnp.int32, (P, P), 0) == lax.broadcasted_iota(jnp.int32, (P, P), 1)).astype(F32)
    col = lax.broadcasted_iota(jnp.int32, (1, CHUNK_COLS), 1)
    tile_slot = 2 * (pl.program_id(0) % PAIRS_PER_TILE)
    st_slot = lax.broadcasted_iota(jnp.int32, (1, SSM_TILE_GROUPS * P), 1) // P
    ch_slot = col // C
    wcr_ref[...] = jnp.zeros(wcr_ref.shape, BF16)
    wci_ref[...] = jnp.zeros(wci_ref.shape, BF16)
    for j in range(2):
        lr, li = lr_ref[j], li_ref[j]
        dt = jnp.exp(ldt_ref[j])
        mag = jnp.exp(ell * (lr * dt))
        ph = ell * (li * dt)
        p_re, p_im = mag * jnp.cos(ph), mag * jnp.sin(ph)
        a_re, a_im = p_re[1:2], p_im[1:2]
        a1r_ref[j], a1i_ref[j] = a_re, a_im
        acr_ref[j], aci_ref[j] = p_re[TL:TL + 1], p_im[TL:TL + 1]
        den = lr * lr + li * li
        nr, ni = a_re - 1.0, a_im
        co_re = (nr * lr + ni * li) / den
        co_im = (ni * lr - nr * li) / den
        b_re, b_im = bre_ref[j], bim_ref[j]
        bb_re = co_re * b_re - co_im * b_im
        bb_im = co_re * b_im + co_im * b_re
        c_re, c_im = cre_ref[j], cim_ref[j]
        g_re = jnp.concatenate([c_re * p_re[l:l + 1] - c_im * p_im[l:l + 1] for l in range(TL + 1)], axis=0)
        g_im = jnp.concatenate([c_re * p_im[l:l + 1] + c_im * p_re[l:l + 1] for l in range(TL + 1)], axis=0)
        kt = (lax.dot_general(bb_re, g_re[:CHUNK_COLS], _NT, precision=hi, preferred_element_type=F32)
              - lax.dot_general(bb_im, g_im[:CHUNK_COLS], _NT, precision=hi, preferred_element_type=F32))
        for s in range(TL):
            rows = slice(s * C, (s + 1) * C)
            shifted = pltpu.roll(kt, s * C, 1) if s else kt
            toep_ref[j, rows, :] = jnp.where(col >= s * C, shifted, 0.0).astype(BF16)
            e = TL - 1 - s
            wsr_ref[j, rows, :] = (p_re[e:e + 1] * bb_re - p_im[e:e + 1] * bb_im).astype(BF16)
            wsi_ref[j, rows, :] = (p_re[e:e + 1] * bb_im + p_im[e:e + 1] * bb_re).astype(BF16)
        blk = (slice(j * P, (j + 1) * P), slice(j * CHUNK_COLS, (j + 1) * CHUNK_COLS))
        wcr_ref[(0,) + blk] = lax.dot_general(eye, g_re[C:], _NT, precision=hi,
                                              preferred_element_type=F32).astype(BF16)
        wci_ref[(0,) + blk] = (-lax.dot_general(eye, g_im[C:], _NT, precision=hi,
                                                preferred_element_type=F32)).astype(BF16)
        mine = st_slot == tile_slot + j
        brow = slice(j * C, (j + 1) * C)
        bdr_ref[0, brow, :] = jnp.where(mine, jnp.tile(bb_re, (1, SSM_TILE_GROUPS)), 0.0).astype(BF16)
        bdi_ref[0, brow, :] = jnp.where(mine, jnp.tile(bb_im, (1, SSM_TILE_GROUPS)), 0.0).astype(BF16)
        mine_c = ch_slot == tile_slot + j
        crow = slice(j * P, (j + 1) * P)
        ct_re = lax.dot_general(eye, jnp.tile(c_re, (SSM_TILE_GROUPS, 1)), _NT, precision=hi, preferred_element_type=F32)
        ct_im = lax.dot_general(eye, jnp.tile(c_im, (SSM_TILE_GROUPS, 1)), _NT, precision=hi, preferred_element_type=F32)
        cdr_ref[0, crow, :] = jnp.where(mine_c, ct_re, 0.0).astype(BF16)
        cdi_ref[0, crow, :] = jnp.where(mine_c, -ct_im, 0.0).astype(BF16)


def _ssm_prep(lam_re, lam_im, log_dt, b_re, b_im, c_re, c_im):
    G, P, C = SSM_GROUPS, SSM_STATE, SSM_GROUP
    n_tiles = G // SSM_TILE_GROUPS
    pair = lambda *s: pl.BlockSpec((2,) + s, lambda q: (q, 0, 0))
    tile = lambda *s: pl.BlockSpec((1,) + s, lambda q: (q // PAIRS_PER_TILE, q % PAIRS_PER_TILE, 0))
    wc_spec = pl.BlockSpec((1, PAIR_STATES, 2 * CHUNK_COLS), lambda q: (q, 0, 0))
    outs = dict(
        toep=(pair(CHUNK_COLS, CHUNK_COLS), (G, CHUNK_COLS, CHUNK_COLS), BF16),
        ws_re=(pair(CHUNK_COLS, P), (G, CHUNK_COLS, P), BF16), ws_im=(pair(CHUNK_COLS, P), (G, CHUNK_COLS, P), BF16),
        wc_re=(wc_spec, (G // 2, PAIR_STATES, 2 * CHUNK_COLS), BF16),
        wc_im=(wc_spec, (G // 2, PAIR_STATES, 2 * CHUNK_COLS), BF16),
        a1_re=(pair(1, P), (G, 1, P), F32), a1_im=(pair(1, P), (G, 1, P), F32),
        ac_re=(pair(1, P), (G, 1, P), F32), ac_im=(pair(1, P), (G, 1, P), F32),
        bd_re=(tile(2 * C, SSM_TILE_GROUPS * P), (n_tiles, SSM_TILE_GROUPS * C, SSM_TILE_GROUPS * P), BF16),
        bd_im=(tile(2 * C, SSM_TILE_GROUPS * P), (n_tiles, SSM_TILE_GROUPS * C, SSM_TILE_GROUPS * P), BF16),
        cd_re=(tile(2 * P, SSM_TILE_GROUPS * C), (n_tiles, SSM_TILE_GROUPS * P, SSM_TILE_GROUPS * C), BF16),
        cd_im=(tile(2 * P, SSM_TILE_GROUPS * C), (n_tiles, SSM_TILE_GROUPS * P, SSM_TILE_GROUPS * C), BF16))
    res = pl.pallas_call(
        _ssm_prep_kernel,
        grid=(G // 2,),
        in_specs=[pair(1, P), pair(1, P), pair(1, 1), pair(C, P), pair(C, P), pair(C, P), pair(C, P)],
        out_specs=[v[0] for v in outs.values()],
        out_shape=[jax.ShapeDtypeStruct(v[1], v[2]) for v in outs.values()],
        compiler_params=_cparams(1),
    )(lam_re.reshape(G, 1, P), lam_im.reshape(G, 1, P), log_dt.reshape(G, 1, 1),
      c_re, c_im, jnp.swapaxes(b_re, 1, 2), jnp.swapaxes(b_im, 1, 2))
    return dict(zip(outs, res))


SSM_STEP_GROUPS = 4


def _ssm_state_kernel(uc_ref, wsr_ref, wsi_ref, sre_ref, sim_ref):
    def all_groups(w_ref):
        return jnp.concatenate([jnp.dot(uc_ref[j], w_ref[j], preferred_element_type=F32)
                                for j in range(SSM_STEP_GROUPS)], axis=-1)
    sre_ref[...] = all_groups(wsr_ref)
    sim_ref[...] = all_groups(wsi_ref)


def _ssm_scan_kernel(ar_ref, ai_ref, sre_ref, sim_ref, hpr_ref, hpi_ref, hfr_ref, hfi_ref):
    ar, ai = ar_ref[...], ai_ref[...]
    batch, n_chunks, width = sre_ref.shape

    def body(k, carry):
        nxt = []
        for b in range(batch):
            hr, hi = carry[2 * b], carry[2 * b + 1]
            row = (b, pl.ds(k, 1), slice(None))
            hpr_ref[row] = hr
            hpi_ref[row] = hi
            nxt += [ar * hr - ai * hi + sre_ref[row], ar * hi + ai * hr + sim_ref[row]]
        return tuple(nxt)

    zero = jnp.zeros((1, width), F32)
    fin = lax.fori_loop(0, n_chunks, body, (zero,) * (2 * batch))
    for b in range(batch):
        hfr_ref[b:b + 1, :] = fin[2 * b]
        hfi_ref[b:b + 1, :] = fin[2 * b + 1]


def _ssm_out_kernel(uc_ref, toep_ref, hpr_ref, hpi_ref, wcr_ref, wci_ref, y_ref):
    for pr in range(SSM_STEP_GROUPS // 2):
        st = slice(pr * PAIR_STATES, (pr + 1) * PAIR_STATES)
        y = jnp.concatenate([jnp.dot(uc_ref[2 * pr + j], toep_ref[2 * pr + j], preferred_element_type=F32)
                             for j in range(2)], axis=-1)
        y += jnp.dot(hpr_ref[:, st].astype(BF16), wcr_ref[pr], preferred_element_type=F32)
        y += jnp.dot(hpi_ref[:, st].astype(BF16), wci_ref[pr], preferred_element_type=F32)
        y_ref[pr] = y.astype(BF16)


def _prompt_ssm(uc, prep, *, batch, seq):
    G, P = SSM_GROUPS, SSM_STATE
    nk = seq // SSM_CHUNK
    rows = nk * batch

    sg = SSM_STEP_GROUPS
    pair3 = lambda a, b: pl.BlockSpec((sg, a, b), lambda q: (q, 0, 0))
    s_spec = pl.BlockSpec((rows, sg * P), lambda q: (0, q))
    s_re, s_im = pl.pallas_call(
        _ssm_state_kernel,
        grid=(G // sg,),
        in_specs=[pair3(rows, CHUNK_COLS), pair3(CHUNK_COLS, P), pair3(CHUNK_COLS, P)],
        out_specs=[s_spec, s_spec],
        out_shape=[jax.ShapeDtypeStruct((rows, G * P), F32)] * 2,
        compiler_params=_cparams(1),
    )(uc, prep['ws_re'], prep['ws_im'])

    wb = 512
    a_spec = pl.BlockSpec((1, wb), lambda j: (0, j))
    seq_spec = pl.BlockSpec((batch, nk, wb), lambda j: (0, 0, j))
    fin_spec = pl.BlockSpec((batch, wb), lambda j: (0, j))
    hp_re, hp_im, hf_re, hf_im = pl.pallas_call(
        _ssm_scan_kernel,
        grid=(G * P // wb,),
        in_specs=[a_spec, a_spec, seq_spec, seq_spec],
        out_specs=[seq_spec, seq_spec, fin_spec, fin_spec],
        out_shape=[jax.ShapeDtypeStruct((batch, nk, G * P), F32)] * 2 + [jax.ShapeDtypeStruct((batch, G * P), F32)] * 2,
        compiler_params=_cparams(1),
    )(prep['ac_re'].reshape(1, G * P), prep['ac_im'].reshape(1, G * P),
      s_re.reshape(batch, nk, G * P), s_im.reshape(batch, nk, G * P))

    yc = pl.pallas_call(
        _ssm_out_kernel,
        grid=(G // sg,),
        in_specs=[pair3(rows, CHUNK_COLS), pair3(CHUNK_COLS, CHUNK_COLS), s_spec, s_spec,
                  pl.BlockSpec((sg // 2, PAIR_STATES, 2 * CHUNK_COLS), lambda q: (q, 0, 0)),
                  pl.BlockSpec((sg // 2, PAIR_STATES, 2 * CHUNK_COLS), lambda q: (q, 0, 0))],
        out_specs=pl.BlockSpec((sg // 2, rows, 2 * CHUNK_COLS), lambda q: (q, 0, 0)),
        out_shape=jax.ShapeDtypeStruct((G // 2, rows, 2 * CHUNK_COLS), BF16),
        compiler_params=_cparams(1),
    )(uc, prep['toep'], hp_re.reshape(rows, G * P), hp_im.reshape(rows, G * P), prep['wc_re'], prep['wc_im'])
    return yc, hf_re, hf_im


def _ssm_step_kernel(u_ref, h0r_ref, h0i_ref, ar_ref, ai_ref, bdr_ref, bdi_ref, cdr_ref, cdi_ref,
                     hr_ref, hi_ref, y_ref):
    ch = SSM_TILE_GROUPS * SSM_GROUP
    st = SSM_TILE_GROUPS * SSM_STATE
    for t in range(SSM_GROUPS // SSM_TILE_GROUPS):
        cs, ss = slice(t * ch, (t + 1) * ch), slice(t * st, (t + 1) * st)
        ub = u_ref[:, cs].astype(BF16)
        ar, ai, h0r, h0i = ar_ref[:, ss], ai_ref[:, ss], h0r_ref[:, ss], h0i_ref[:, ss]
        hr = ar * h0r - ai * h0i + jnp.dot(ub, bdr_ref[t], preferred_element_type=F32)
        hi = ar * h0i + ai * h0r + jnp.dot(ub, bdi_ref[t], preferred_element_type=F32)
        hr_ref[:, ss] = hr
        hi_ref[:, ss] = hi
        y_ref[:, cs] = (jnp.dot(hr.astype(BF16), cdr_ref[t], preferred_element_type=F32)
                        + jnp.dot(hi.astype(BF16), cdi_ref[t], preferred_element_type=F32))


def _sample_ssm(u, h0_re, h0_im, prep):
    bsz = u.shape[0]
    n_st = SSM_STATES
    args = (u, h0_re.reshape(bsz, n_st), h0_im.reshape(bsz, n_st),
            prep['a1_re'].reshape(1, n_st), prep['a1_im'].reshape(1, n_st),
            prep['bd_re'], prep['bd_im'], prep['cd_re'], prep['cd_im'])
    return pl.pallas_call(
        _ssm_step_kernel,
        grid=(1,),
        in_specs=[_const_spec(a.shape) for a in args],
        out_specs=[pl.BlockSpec((bsz, n_st), lambda i: (0, 0))] * 2 + [pl.BlockSpec((bsz, D_MODEL), lambda i: (0, 0))],
        out_shape=[jax.ShapeDtypeStruct((bsz, n_st), F32)] * 2 + [jax.ShapeDtypeStruct((bsz, D_MODEL), F32)],
        compiler_params=_cparams(1),
    )(*args)


FF_CHUNK = D_FF // 4


def _merge_kernel(x_ref, o_ref, ga_ref, gs_ref, y_ref, u_ref, g1_ref, sc2_ref, sh2_ref, g2_ref,
                  d_ref, wglu_ref, bglu_ref, wo_ref, ln1g_ref, ln1b_ref,
                  wup_ref, bup_ref, wdn_ref, bdn_ref, ln2g_ref, ln2b_ref, out_ref, x1_s, h2_s, *maybe_y_s):
    i = pl.program_id(0)
    slot = i % 2

    @pl.when(i == 0)
    def _():
        x1_s[1] = jnp.zeros(x1_s.shape[1:], F32)
        h2_s[1] = jnp.zeros(h2_s.shape[1:], BF16)

    x1p = x1_s[1 - slot]
    h2p = h2_s[1 - slot]

    def mlp_chunk(c):
        cs = slice(c * FF_CHUNK, (c + 1) * FF_CHUNK)
        a = jnp.maximum(jnp.dot(h2p, wup_ref[:, cs], preferred_element_type=F32) + bup_ref[:, cs], 0.0)
        return jnp.dot((a * a).astype(BF16), wdn_ref[cs, :], preferred_element_type=F32)

    f = mlp_chunk(0)
    if maybe_y_s:
        y_s = maybe_y_s[0]
        _from_chunk_layout(y_ref, y_s)
        y = jnp.concatenate([y_s[jb] for jb in range(D_MODEL // LANES)], axis=-1)
    else:
        y = y_ref[...]
    yd = y + d_ref[...] * u_ref[...]
    g = jax.nn.gelu(yd, approximate=True)
    f += mlp_chunk(1)
    z = jnp.dot(g.astype(BF16), wglu_ref[...], preferred_element_type=F32) + bglu_ref[...]
    y_ssm = g * jax.nn.sigmoid(z)
    mixed = (jax.nn.sigmoid(ga_ref[...].astype(F32)) * o_ref[...].astype(F32)
             + jax.nn.sigmoid(gs_ref[...].astype(F32)) * y_ssm)
    f += mlp_chunk(2)
    t = jnp.dot(mixed.astype(BF16), wo_ref[...], preferred_element_type=F32)
    x1 = _layer_norm(ALPHA * x_ref[...] + (1.0 + g1_ref[0]) * t) * ln1g_ref[...] + ln1b_ref[...]
    x1_s[slot] = x1
    f += mlp_chunk(3) + bdn_ref[...]
    h2_s[slot] = (_layer_norm(x1) * (1.0 + sc2_ref[0]) + sh2_ref[0]).astype(BF16)
    out_ref[...] = _layer_norm(ALPHA * x1p + (1.0 + g2_ref[0]) * f) * ln2g_ref[...] + ln2b_ref[...]


def _merge(x, o, ga, gs, y, u, mods, params, *, tm, rows_per_mod):
    T = x.shape[0]
    n = T // tm
    mod_rows = mods[0].shape[1]
    cur = lambda i: jnp.minimum(i, n - 1)
    prev = lambda i: jnp.maximum(i - 1, 0)
    mod_cur = pl.BlockSpec((1, mod_rows, D_MODEL), lambda i: (cur(i) * tm // rows_per_mod, 0, 0))
    mod_prev = pl.BlockSpec((1, mod_rows, D_MODEL), lambda i: (prev(i) * tm // rows_per_mod, 0, 0))
    row = pl.BlockSpec((tm, D_MODEL), lambda i: (cur(i), 0))
    chunked = y.ndim == 3
    y_spec = (pl.BlockSpec((SSM_GROUPS // 2, tm // SSM_CHUNK, 2 * CHUNK_COLS), lambda i: (0, cur(i), 0))
              if chunked else row)
    scratch = [pltpu.VMEM((2, tm, D_MODEL), F32), pltpu.VMEM((2, tm, D_MODEL), BF16)]
    if chunked:
        scratch.append(pltpu.VMEM((D_MODEL // LANES, tm, LANES), F32))
    return pl.pallas_call(
        _merge_kernel,
        grid=(n + 1,),
        in_specs=[row] * 4 + [y_spec, row] + [mod_cur] * 3 + [mod_prev] + [_const_spec(p.shape) for p in params],
        out_specs=pl.BlockSpec((tm, D_MODEL), lambda i: (prev(i), 0)),
        out_shape=jax.ShapeDtypeStruct((T, D_MODEL), F32),
        scratch_shapes=scratch,
        compiler_params=_cparams(1),
    )(x, o, ga, gs, y, u, *mods, *params)


def _split_mods(ada):
    return [ada[..., i * D_MODEL:(i + 1) * D_MODEL] for i in range(6)]


def kernel(x_prompt, x_sample, cache_k, cache_v, state_ssm_re, state_ssm_im, page_table, c_prompt, c_sample, w_in, lambda_q1, lambda_k1, lambda_q2, lambda_k2, subln_w, ssm_lambda_re, ssm_lambda_im, ssm_log_dt, ssm_b_re, ssm_b_im, ssm_c_re, ssm_c_im, ssm_d, w_glu, b_glu, w_o, w_ada, b_ada, ln1_g, ln1_b, ln2_g, ln2_b, w_up, b_up, w_down, b_down):
    assert w_in.shape[0] == DEPTH and x_sample.shape[1] == 1
    batch, seq, _ = x_prompt.shape
    dec_b = x_sample.shape[0]
    page = cache_k.shape[2]
    past_len = page_table.shape[1] * page
    l = 0
    row2 = lambda a: a[l].reshape(1, -1)

    pad = (-batch) % 8
    c_all = jnp.concatenate([c_prompt, jnp.zeros((pad, D_MODEL), F32), c_sample], axis=0)
    ada = _ada(c_all, w_ada[l], row2(b_ada))
    mods_p = _split_mods(ada[:batch].reshape(batch, 1, 6 * D_MODEL))
    mods_s = _split_mods(ada[batch + pad:].reshape(1, dec_b, 6 * D_MODEL))

    w_in_b = w_in[l].astype(BF16)
    lams = (row2(lambda_q1), row2(lambda_k1), row2(lambda_q2), row2(lambda_k2))
    subw = row2(subln_w)
    merge_params = (row2(ssm_d), w_glu[l].astype(BF16), row2(b_glu), w_o[l].astype(BF16), row2(ln1_g), row2(ln1_b),
                    w_up[l].astype(BF16), row2(b_up), w_down[l].astype(BF16), row2(b_down), row2(ln2_g), row2(ln2_b))
    prep = _ssm_prep(ssm_lambda_re[l], ssm_lambda_im[l], ssm_log_dt[l], ssm_b_re[l], ssm_b_im[l],
                     ssm_c_re[l], ssm_c_im[l])

    tm = 512
    T = batch * seq
    xp = x_prompt.reshape(T, D_MODEL)
    q, k, v, kb, vb, u, ga, gs, uc = _proj(xp, mods_p[1], mods_p[0], _rope_tables(jnp.arange(seq)), w_in_b,
                                           tm=tm, rows_per_mod=seq, table_blocks=seq // tm, chunk_u=True)
    o = _prompt_attention(q, kb, vb, lams, subw, batch=batch, seq=seq, tq=512, tk=512)
    yc, hf_re, hf_im = _prompt_ssm(uc, prep, batch=batch, seq=seq)
    yp = _merge(xp, o, ga, gs, yc, u, (mods_p[2], mods_p[4], mods_p[3], mods_p[5]), merge_params,
                tm=256, rows_per_mod=seq)

    xs = x_sample.reshape(dec_b, D_MODEL)
    pos_s = jnp.full((dec_b,), past_len, jnp.int32)
    qs, ks, vs, _, _, us, gas, gss = _proj(xs, mods_s[1], mods_s[0], _rope_tables(pos_s), w_in_b,
                                           tm=dec_b, rows_per_mod=dec_b, table_blocks=1, chunk_u=False)
    os_ = _decode_attention(qs, ks, vs, cache_k, cache_v, page_table, lams, subw, l).astype(BF16)
    hs_re, hs_im, y_s = _sample_ssm(us, state_ssm_re[l], state_ssm_im[l], prep)
    ys = _merge(xs, os_, gas, gss, y_s, us, (mods_s[2], mods_s[4], mods_s[3], mods_s[5]), merge_params,
                tm=dec_b, rows_per_mod=dec_b)

    st = (SSM_GROUPS, SSM_STATE)
    return (yp.reshape(batch, seq, D_MODEL), ys.reshape(dec_b, 1, D_MODEL),
            k.reshape(1, batch, seq, N_KV_HEADS, QK_DIM), v.reshape(1, batch, seq, N_KV_HEADS, V_DIM),
            hf_re.reshape(1, batch, *st), hf_im.reshape(1, batch, *st),
            ks.reshape(1, dec_b, 1, N_KV_HEADS, QK_DIM), vs.reshape(1, dec_b, 1, N_KV_HEADS, V_DIM),
            hs_re.reshape(1, dec_b, *st), hs_im.reshape(1, dec_b, *st))
```
